```python
import math
import jax, jax.numpy as jnp
from jax import lax
import numpy as np

D_MODEL = 1024
BATCH = 2
SEQ = 8192
DEPTH = 1

MIX_WIDTH = D_MODEL
ATTN_HEADS = 4
QK_HEAD_DIM = 64
V_HEAD_DIM = 2 * QK_HEAD_DIM
ATTN_WIDTH = ATTN_HEADS * V_HEAD_DIM
QK_WIDTH = ATTN_HEADS * 2 * QK_HEAD_DIM
ROT_DIM = QK_HEAD_DIM // 4
ROPE_THETA = 500000.0
QBLOCK = 128
RNN_WIDTH = MIX_WIDTH - ATTN_WIDTH
RNN_HEADS = 8
RNN_BLOCK = RNN_WIDTH // RNN_HEADS
RNN_CONV_WIDTH = 4
LRU_C = 8.0
IN_PROJ_WIDTH = 2 * QK_WIDTH + ATTN_WIDTH + 2 * RNN_WIDTH
D_FF = 3 * D_MODEL
FFN_CONV_WIDTH = 3
EPS = 1e-6

kernel_name = "hybrid_diffattn_rglru_convffn"


def rms_norm(x, w):
    xf = x.astype(jnp.float32)
    y = xf * lax.rsqrt(jnp.mean(xf * xf, axis=-1, keepdims=True) + EPS)
    return (y * w.astype(jnp.float32)).astype(x.dtype)


def causal_dwconv(x, w, b):
    width = w.shape[0]
    c = x.shape[-1]
    y = lax.conv_general_dilated(
        x, w[:, None, :].astype(x.dtype), window_strides=(1,),
        padding=[(width - 1, 0)], dimension_numbers=('NWC', 'WIO', 'NWC'),
        feature_group_count=c)
    return y + b.astype(x.dtype)


def rope_tables(seq):
    pos = jnp.arange(seq, dtype=jnp.float32)
    inv_freq = ROPE_THETA ** (-(jnp.arange(0, ROT_DIM, 2, dtype=jnp.float32) / ROT_DIM))
    ang = pos[:, None] * inv_freq[None, :]
    return jnp.cos(ang), jnp.sin(ang)


def partial_rope(t, cos, sin):
    c = cos[None, :, None, None, :].astype(t.dtype)
    s = sin[None, :, None, None, :].astype(t.dtype)
    half = ROT_DIM // 2
    x1 = t[..., :half]
    x2 = t[..., half:ROT_DIM]
    rot = jnp.concatenate([x1 * c - x2 * s, x2 * c + x1 * s], axis=-1)
    return jnp.concatenate([rot, t[..., ROT_DIM:]], axis=-1)


def diff_attention(q, k, v, lam):
    b, s, h, _, dh = q.shape
    dv = v.shape[-1]
    qt = q.transpose(0, 2, 3, 1, 4)
    kt = k.transpose(0, 2, 3, 1, 4)
    vt = v.transpose(0, 2, 1, 3)
    n_blocks = s // QBLOCK
    kpos = jnp.arange(s)
    scale = dh ** -0.5

    def block(i):
        start = i * QBLOCK
        qb = lax.dynamic_slice_in_dim(qt, start, QBLOCK, axis=3)
        sc = jnp.einsum('bhcqd,bhckd->bhcqk', qb, kt).astype(jnp.float32) * scale
        qpos = start + jnp.arange(QBLOCK)
        mask = kpos[None, :] <= qpos[:, None]
        sc = jnp.where(mask, sc, -jnp.inf)
        p = jax.nn.softmax(sc, axis=-1)
        a = p[:, :, 0] - lam * p[:, :, 1]
        return jnp.einsum('bhqk,bhkv->bhqv', a.astype(vt.dtype), vt)

    out = lax.map(block, jnp.arange(n_blocks))
    return out.transpose(1, 0, 3, 2, 4).reshape(b, s, h, dv)


def rg_lru(x, wa, ba, wx, bx, lru_lambda):
    b, s, c = x.shape
    xb = x.reshape(b, s, RNN_HEADS, RNN_BLOCK)
    r = jax.nn.sigmoid(jnp.einsum('bshi,hij->bshj', xb, wa) + ba).reshape(b, s, c).astype(jnp.float32)
    ig = jax.nn.sigmoid(jnp.einsum('bshi,hij->bshj', xb, wx) + bx).reshape(b, s, c).astype(jnp.float32)
    log_a = -LRU_C * r * jax.nn.softplus(-lru_lambda.astype(jnp.float32))
    a = jnp.exp(log_a)
    mult = jnp.sqrt(-jnp.expm1(2.0 * log_a))
    u = mult * ig * x.astype(jnp.float32)

    def combine(left, right):
        a1, b1 = left
        a2, b2 = right
        return a1 * a2, a2 * b1 + b2

    _, hs = lax.associative_scan(combine, (a, u), axis=1)
    return hs.astype(x.dtype)


def setup_inputs(seed: int = 0) -> dict:
    key = jax.random.key(seed)
    ks = jax.random.split(key, 32)
    L = DEPTH

    def nrm(k, shape, scale):
        return jax.random.normal(k, shape, jnp.float32) * scale

    u = jax.random.uniform(ks[14], (L, RNN_WIDTH), jnp.float32, minval=0.9, maxval=0.999)
    sig = u ** (1.0 / LRU_C)
    lru_lambda = jnp.log(sig) - jnp.log1p(-sig)
    return {
        "x": nrm(ks[0], (BATCH, SEQ, D_MODEL), 1.0),
        "norm1_w": 1.0 + nrm(ks[1], (L, D_MODEL), 0.01),
        "w_in": nrm(ks[2], (L, D_MODEL, IN_PROJ_WIDTH), D_MODEL ** -0.5),
        "q_norm_w": 1.0 + nrm(ks[3], (L, QK_HEAD_DIM), 0.01),
        "k_norm_w": 1.0 + nrm(ks[4], (L, QK_HEAD_DIM), 0.01),
        "lambda_q1": nrm(ks[5], (L, QK_HEAD_DIM), 0.1),
        "lambda_k1": nrm(ks[6], (L, QK_HEAD_DIM), 0.1),
        "lambda_q2": nrm(ks[7], (L, QK_HEAD_DIM), 0.1),
        "lambda_k2": nrm(ks[8], (L, QK_HEAD_DIM), 0.1),
        "subln_w": 1.0 + nrm(ks[9], (L, V_HEAD_DIM), 0.01),
        "conv_rnn_w": nrm(ks[10], (L, RNN_CONV_WIDTH, RNN_WIDTH), RNN_CONV_WIDTH ** -0.5),
        "conv_rnn_b": nrm(ks[11], (L, RNN_WIDTH), 0.01),
        "w_gate_a": nrm(ks[12], (L, RNN_HEADS, RNN_BLOCK, RNN_BLOCK), RNN_BLOCK ** -0.5),
        "b_gate_a": nrm(ks[13], (L, RNN_HEADS, RNN_BLOCK), 0.01),
        "w_gate_x": nrm(ks[15], (L, RNN_HEADS, RNN_BLOCK, RNN_BLOCK), RNN_BLOCK ** -0.5),
        "b_gate_x": nrm(ks[16], (L, RNN_HEADS, RNN_BLOCK), 0.01),
        "lru_lambda": lru_lambda,
        "rnn_norm_w": 1.0 + nrm(ks[17], (L, RNN_WIDTH), 0.01),
        "w_out": nrm(ks[18], (L, MIX_WIDTH, D_MODEL), MIX_WIDTH ** -0.5),
        "norm2_w": 1.0 + nrm(ks[19], (L, D_MODEL), 0.01),
        "w_up": nrm(ks[20], (L, D_MODEL, 2 * D_FF), D_MODEL ** -0.5),
        "conv_ffn_w": nrm(ks[21], (L, FFN_CONV_WIDTH, 2 * D_FF), FFN_CONV_WIDTH ** -0.5),
        "conv_ffn_b": nrm(ks[22], (L, 2 * D_FF), 0.01),
        "w_down": nrm(ks[23], (L, D_FF, D_MODEL), D_FF ** -0.5),
    }


def reference(x, norm1_w, w_in, q_norm_w, k_norm_w, lambda_q1, lambda_k1, lambda_q2,
              lambda_k2, subln_w, conv_rnn_w, conv_rnn_b, w_gate_a, b_gate_a, w_gate_x,
              b_gate_x, lru_lambda, rnn_norm_w, w_out, norm2_w, w_up, conv_ffn_w,
              conv_ffn_b, w_down):
    b, s, _ = x.shape
    cos, sin = rope_tables(s)
    for l in range(DEPTH):
        lambda_init = 0.8 - 0.6 * math.exp(-0.3 * l)
        h = rms_norm(x, norm1_w[l])
        proj = h @ w_in[l]
        q, k, v, xr, gr = jnp.split(
            proj, [QK_WIDTH, 2 * QK_WIDTH, 2 * QK_WIDTH + ATTN_WIDTH,
                   2 * QK_WIDTH + ATTN_WIDTH + RNN_WIDTH], axis=-1)
        q = q.reshape(b, s, ATTN_HEADS, 2, QK_HEAD_DIM)
        k = k.reshape(b, s, ATTN_HEADS, 2, QK_HEAD_DIM)
        v = v.reshape(b, s, ATTN_HEADS, V_HEAD_DIM)
        q = partial_rope(rms_norm(q, q_norm_w[l]), cos, sin)
        k = partial_rope(rms_norm(k, k_norm_w[l]), cos, sin)
        lam = (jnp.exp(jnp.sum(lambda_q1[l].astype(jnp.float32) * lambda_k1[l].astype(jnp.float32)))
               - jnp.exp(jnp.sum(lambda_q2[l].astype(jnp.float32) * lambda_k2[l].astype(jnp.float32)))
               + lambda_init)
        attn = diff_attention(q, k, v, lam)
        attn = rms_norm(attn, subln_w[l]) * (1.0 - lambda_init)
        attn = attn.reshape(b, s, ATTN_WIDTH)
        xc = causal_dwconv(xr, conv_rnn_w[l], conv_rnn_b[l])
        y = rg_lru(xc, w_gate_a[l], b_gate_a[l], w_gate_x[l], b_gate_x[l], lru_lambda[l])
        y = rms_norm(y * jax.nn.gelu(gr), rnn_norm_w[l])
        x = x + jnp.concatenate([attn, y], axis=-1) @ w_out[l]
        h2 = rms_norm(x, norm2_w[l])
        up = causal_dwconv(h2 @ w_up[l], conv_ffn_w[l], conv_ffn_b[l])
        g, val = jnp.split(up, 2, axis=-1)
        x = x + (jax.nn.gelu(g) * val) @ w_down[l]
    return x
```

```python
import functools
import math

import jax
import jax.numpy as jnp
from jax import lax
from jax.experimental import pallas as pl
from jax.experimental.pallas import tpu as pltpu

D_MODEL = 1024
DEPTH = 1
ATTN_HEADS = 4
QK_HEAD_DIM = 64
V_HEAD_DIM = 2 * QK_HEAD_DIM
ATTN_WIDTH = ATTN_HEADS * V_HEAD_DIM
QK_WIDTH = ATTN_HEADS * 2 * QK_HEAD_DIM
ROT_DIM = QK_HEAD_DIM // 4
ROPE_THETA = 500000.0
RNN_WIDTH = D_MODEL - ATTN_WIDTH
RNN_HEADS = 8
RNN_BLOCK = RNN_WIDTH // RNN_HEADS
RNN_CONV_WIDTH = 4
LRU_C = 8.0
D_FF = 3 * D_MODEL
FFN_CONV_WIDTH = 3
EPS = 1e-6

LANES = 128
SUBLANES_F32 = 8
SUBLANES_BF16 = 16
VMEM_LIMIT_BYTES = 56 * 1024 * 1024

PROJ_ROWS = 512
ATTN_BLOCK = 256
RNN_ROWS = 512
FFN_ROWS = 512
FFN_COLS = 512

LOG2E = math.log2(math.e)
F32 = jnp.float32
BF16 = jnp.bfloat16


def _gelu_tanh(x):
    c = math.sqrt(2.0 / math.pi)
    return 0.5 * x * (1.0 + jnp.tanh(c * (x + 0.044715 * (x * x * x))))


def _in_proj_kernel(x_ref, n1_ref, w_ref, qw_ref, kw_ref, rc_ref, rs1_ref, rs2_ref,
                    q_ref, k_ref, v_ref, xr_ref, gr_ref):
    x = x_ref[0]
    ms = jnp.mean(x * x, axis=-1, keepdims=True)
    h = (x * lax.rsqrt(ms + EPS) * n1_ref[...]).astype(BF16)
    lane = lax.broadcasted_iota(jnp.int32, (1, LANES), 1)
    lo = lane < QK_HEAD_DIM
    rc, rs1, rs2 = rc_ref[...], rs1_ref[...], rs2_ref[...]

    def qk_section(col0, nw_ref, out_ref, out_scale):
        p = jnp.dot(h, w_ref[:, col0:col0 + QK_WIDTH], preferred_element_type=F32)
        nw = nw_ref[...]
        for hd in range(ATTN_HEADS):
            y = p[:, LANES * hd:LANES * (hd + 1)]
            sq = y * y
            s_all = jnp.sum(sq, axis=-1, keepdims=True)
            s_lo = jnp.sum(jnp.where(lo, sq, 0.0), axis=-1, keepdims=True)
            msq = jnp.where(lo, s_lo, s_all - s_lo) * (1.0 / QK_HEAD_DIM)
            y = y * lax.rsqrt(msq + EPS) * nw
            up = pltpu.roll(y, LANES - ROT_DIM // 2, axis=1)
            dn = pltpu.roll(y, ROT_DIM // 2, axis=1)
            y = y * rc + up * rs1 + dn * rs2
            if out_scale != 1.0:
                y = y * out_scale
            out_ref[0, :, LANES * hd:LANES * (hd + 1)] = y.astype(BF16)

    qk_section(0, qw_ref, q_ref, LOG2E * QK_HEAD_DIM ** -0.5)
    qk_section(QK_WIDTH, kw_ref, k_ref, 1.0)
    c = 2 * QK_WIDTH
    v_ref[0] = jnp.dot(h, w_ref[:, c:c + ATTN_WIDTH], preferred_element_type=F32).astype(BF16)
    c += ATTN_WIDTH
    xr_ref[0] = jnp.dot(h, w_ref[:, c:c + RNN_WIDTH], preferred_element_type=F32)
    c += RNN_WIDTH
    gr_ref[0] = jnp.dot(h, w_ref[:, c:c + RNN_WIDTH], preferred_element_type=F32)


def _in_proj(x, n1, w_in, qw2, kw2, rc, rs1, rs2):
    b, s, d = x.shape
    tm = PROJ_ROWS
    n_out = w_in.shape[1]
    row_blk = lambda w: pl.BlockSpec((1, tm, w), lambda bi, i: (bi, i, 0))
    const = lambda shape: pl.BlockSpec(shape, lambda bi, i: (0,) * len(shape))
    rope_blk = pl.BlockSpec((tm, LANES), lambda bi, i: (i, 0))
    return pl.pallas_call(
        _in_proj_kernel,
        grid=(b, s // tm),
        in_specs=[row_blk(d), const((1, d)), const((d, n_out)), const((1, LANES)),
                  const((1, LANES)), rope_blk, rope_blk, rope_blk],
        out_specs=[row_blk(QK_WIDTH), row_blk(QK_WIDTH), row_blk(ATTN_WIDTH),
                   row_blk(RNN_WIDTH), row_blk(RNN_WIDTH)],
        out_shape=[jax.ShapeDtypeStruct((b, s, QK_WIDTH), BF16),
                   jax.ShapeDtypeStruct((b, s, QK_WIDTH), BF16),
                   jax.ShapeDtypeStruct((b, s, ATTN_WIDTH), BF16),
                   jax.ShapeDtypeStruct((b, s, RNN_WIDTH), F32),
                   jax.ShapeDtypeStruct((b, s, RNN_WIDTH), F32)],
        compiler_params=pltpu.CompilerParams(
            dimension_semantics=("arbitrary", "arbitrary"),
            vmem_limit_bytes=VMEM_LIMIT_BYTES),
        name="in_proj",
    )(x, n1, w_in, qw2, kw2, rc, rs1, rs2)


def _attn_kernel(q_ref, k_ref, v_ref, lq1_ref, lk1_ref, lq2_ref, lk2_ref, sw_ref, o_ref,
                 *, lambda_init):
    t = ATTN_BLOCK
    qi = pl.program_id(2)
    q = q_ref[0]
    qs = (q[:, :QK_HEAD_DIM], q[:, QK_HEAD_DIM:])

    def block(j, carry, masked):
        r0 = pl.multiple_of(j * t, t)
        kb = k_ref[0, pl.ds(r0, t), :]
        vb = v_ref[0, pl.ds(r0, t), :]
        out = []
        for c in range(2):
            m, l, acc = carry[c]
            kc = kb[:, c * QK_HEAD_DIM:(c + 1) * QK_HEAD_DIM]
            s = lax.dot_general(qs[c], kc, (((1,), (1,)), ((), ())),
                                preferred_element_type=F32)
            if masked:
                row = lax.broadcasted_iota(jnp.int32, (t, t), 0)
                col = lax.broadcasted_iota(jnp.int32, (t, t), 1)
                s = jnp.where(col <= row, s, -jnp.inf)
            m_new = jnp.maximum(m, jnp.max(s, axis=-1, keepdims=True))
            alpha = jnp.exp2(m - m_new)
            e = jnp.exp2(s - m_new)
            l = alpha * l + jnp.sum(e, axis=-1, keepdims=True)
            acc = alpha * acc + jnp.dot(e.astype(BF16), vb, preferred_element_type=F32)
            out.append((m_new, l, acc))
        return tuple(out)

    init_c = (jnp.full((t, 1), -1e30, F32), jnp.zeros((t, 1), F32),
              jnp.zeros((t, V_HEAD_DIM), F32))
    carry = lax.fori_loop(0, qi, lambda j, c: block(j, c, False), (init_c, init_c))
    (_, l0, a0), (_, l1, a1) = block(qi, carry, True)

    lam = (jnp.exp(jnp.sum(lq1_ref[...] * lk1_ref[...], axis=-1, keepdims=True))
           - jnp.exp(jnp.sum(lq2_ref[...] * lk2_ref[...], axis=-1, keepdims=True))
           + lambda_init)
    o = a0 / l0 - lam * (a1 / l1)
    ms = jnp.mean(o * o, axis=-1, keepdims=True)
    o = o * lax.rsqrt(ms + EPS) * sw_ref[...] * (1.0 - lambda_init)
    o_ref[0] = o.astype(BF16)


def _attention(q, k, v, lq1, lk1, lq2, lk2, sw, lambda_init):
    b, s, _ = q.shape
    t = ATTN_BLOCK
    vec = lambda n: pl.BlockSpec((1, n), lambda bi, h, i: (0, 0))
    return pl.pallas_call(
        functools.partial(_attn_kernel, lambda_init=lambda_init),
        grid=(b, ATTN_HEADS, s // t),
        in_specs=[pl.BlockSpec((1, t, LANES), lambda bi, h, i: (bi, i, h)),
                  pl.BlockSpec((1, s, LANES), lambda bi, h, i: (bi, 0, h)),
                  pl.BlockSpec((1, s, LANES), lambda bi, h, i: (bi, 0, h)),
                  vec(QK_HEAD_DIM), vec(QK_HEAD_DIM), vec(QK_HEAD_DIM), vec(QK_HEAD_DIM),
                  vec(V_HEAD_DIM)],
        out_specs=pl.BlockSpec((1, t, LANES), lambda bi, h, i: (bi, i, h)),
        out_shape=jax.ShapeDtypeStruct((b, s, ATTN_WIDTH), BF16),
        compiler_params=pltpu.CompilerParams(
            dimension_semantics=("arbitrary", "arbitrary", "arbitrary"),
            vmem_limit_bytes=VMEM_LIMIT_BYTES),
        name="diff_attn",
    )(q, k, v, lq1, lk1, lq2, lk2, sw)


def _rglru_kernel(xr_ref, gr_ref, cw_ref, cb_ref, wa_ref, ba_ref, wx_ref, bx_ref, lam_ref,
                  nw_ref, y_ref, ext_ref, a_ref, b_ref, hc_ref):
    ts = RNN_ROWS
    halo = SUBLANES_F32

    @pl.when(pl.program_id(1) == 0)
    def _():
        ext_ref[0:halo, :] = jnp.zeros((halo, RNN_WIDTH), F32)
        hc_ref[...] = jnp.zeros((halo, RNN_WIDTH), F32)

    x = xr_ref[0]
    ext_ref[halo:halo + ts, :] = x
    xc = cb_ref[...] + cw_ref[RNN_CONV_WIDTH - 1:RNN_CONV_WIDTH, :] * x
    for d in range(1, RNN_CONV_WIDTH):
        tap = RNN_CONV_WIDTH - 1 - d
        xc = xc + cw_ref[tap:tap + 1, :] * ext_ref[halo - d:halo - d + ts, :]
    ext_ref[0:halo, :] = x[ts - halo:ts, :]

    xb = xc.astype(BF16)
    r = jax.nn.sigmoid(jnp.dot(xb, wa_ref[...], preferred_element_type=F32) + ba_ref[...])
    ig = jax.nn.sigmoid(jnp.dot(xb, wx_ref[...], preferred_element_type=F32) + bx_ref[...])
    z = -lam_ref[...]
    softplus = jnp.maximum(z, 0.0) + jnp.log1p(jnp.exp(-jnp.abs(z)))
    log_a = -LRU_C * r * softplus
    a = jnp.exp(log_a)
    u = jnp.sqrt(1.0 - a * a) * ig * xc

    sub = lax.broadcasted_iota(jnp.int32, (ts, 1), 0) % SUBLANES_F32
    d = 1
    while d < SUBLANES_F32:
        keep = sub >= d
        a_prev = jnp.where(keep, pltpu.roll(a, d, axis=0), 1.0)
        u_prev = jnp.where(keep, pltpu.roll(u, d, axis=0), 0.0)
        u = a * u_prev + u
        a = a * a_prev
        d *= 2
    a_ref[...] = a
    b_ref[...] = u

    def tile(j, h_prev):
        r0 = pl.multiple_of(j * SUBLANES_F32, SUBLANES_F32)
        h = b_ref[pl.ds(r0, SUBLANES_F32), :] + a_ref[pl.ds(r0, SUBLANES_F32), :] * h_prev
        b_ref[pl.ds(r0, SUBLANES_F32), :] = h
        return jnp.broadcast_to(h[SUBLANES_F32 - 1:SUBLANES_F32, :], h.shape)

    hc_ref[...] = lax.fori_loop(0, ts // SUBLANES_F32, tile, hc_ref[...], unroll=8)

    y = b_ref[...] * _gelu_tanh(gr_ref[0])
    ms = jnp.mean(y * y, axis=-1, keepdims=True)
    y_ref[0] = (y * lax.rsqrt(ms + EPS) * nw_ref[...]).astype(BF16)


def _rglru(xr, gr, cw, cb, wa, ba, wx, bx, lam, nw):
    b, s, c = xr.shape
    ts = RNN_ROWS
    row_blk = pl.BlockSpec((1, ts, c), lambda bi, i: (bi, i, 0))
    const = lambda shape: pl.BlockSpec(shape, lambda bi, i: (0,) * len(shape))
    return pl.pallas_call(
        _rglru_kernel,
        grid=(b, s // ts),
        in_specs=[row_blk, row_blk, const((RNN_CONV_WIDTH, c)), const((1, c)),
                  const((c, c)), const((1, c)), const((c, c)), const((1, c)),
                  const((1, c)), const((1, c))],
        out_specs=row_blk,
        out_shape=jax.ShapeDtypeStruct((b, s, c), BF16),
        scratch_shapes=[pltpu.VMEM((ts + SUBLANES_F32, c), F32),
                        pltpu.VMEM((ts, c), F32), pltpu.VMEM((ts, c), F32),
                        pltpu.VMEM((SUBLANES_F32, c), F32)],
        compiler_params=pltpu.CompilerParams(
            dimension_semantics=("arbitrary", "arbitrary"),
            vmem_limit_bytes=VMEM_LIMIT_BYTES),
        name="rglru",
    )(xr, gr, cw, cb, wa, ba, wx, bx, lam, nw)


def _out_proj_kernel(attn_ref, y_ref, x_ref, w_ref, n2_ref, x1_ref, h2_ref):
    o = jnp.dot(attn_ref[0], w_ref[0:ATTN_WIDTH, :], preferred_element_type=F32)
    o = o + jnp.dot(y_ref[0], w_ref[ATTN_WIDTH:, :], preferred_element_type=F32)
    x1 = x_ref[0] + o
    x1_ref[0] = x1
    ms = jnp.mean(x1 * x1, axis=-1, keepdims=True)
    h2_ref[0] = (x1 * lax.rsqrt(ms + EPS) * n2_ref[...]).astype(BF16)


def _out_proj(attn, y, x, w_out, n2):
    b, s, d = x.shape
    tm = PROJ_ROWS
    row_blk = lambda w: pl.BlockSpec((1, tm, w), lambda bi, i: (bi, i, 0))
    const = lambda shape: pl.BlockSpec(shape, lambda bi, i: (0,) * len(shape))
    return pl.pallas_call(
        _out_proj_kernel,
        grid=(b, s // tm),
        in_specs=[row_blk(ATTN_WIDTH), row_blk(RNN_WIDTH), row_blk(d), const((d, d)),
                  const((1, d))],
        out_specs=[row_blk(d), row_blk(d)],
        out_shape=[jax.ShapeDtypeStruct((b, s, d), F32), jax.ShapeDtypeStruct((b, s, d), BF16)],
        compiler_params=pltpu.CompilerParams(
            dimension_semantics=("arbitrary", "arbitrary"),
            vmem_limit_bytes=VMEM_LIMIT_BYTES),
        name="out_proj",
    )(attn, y, x, w_out, n2)


def _ffn_kernel(h2_ref, halo_ref, x1_ref, wg_ref, wv_ref, cwg_ref, cwv_ref, cbg_ref, cbv_ref,
                wd_ref, o_ref, ug_ref, uv_ref, acc_ref):
    tm = FFN_ROWS
    hr = SUBLANES_BF16
    halo = halo_ref[0]
    halo = jnp.where(pl.program_id(1) > 0, halo, jnp.zeros_like(halo))
    hext = jnp.concatenate([halo, h2_ref[0]], axis=0)
    acc_ref[...] = x1_ref[0]

    def conv(u_ref, cw, cb):
        y = cb + cw[FFN_CONV_WIDTH - 1:FFN_CONV_WIDTH, :] * u_ref[hr:hr + tm, :]
        for d in range(1, FFN_CONV_WIDTH):
            tap = FFN_CONV_WIDTH - 1 - d
            y = y + cw[tap:tap + 1, :] * u_ref[hr - d:hr - d + tm, :]
        return y

    def chunk(j, carry):
        ug_ref[...] = jnp.dot(hext, wg_ref[j], preferred_element_type=F32)
        uv_ref[...] = jnp.dot(hext, wv_ref[j], preferred_element_type=F32)
        g = conv(ug_ref, cwg_ref[j], cbg_ref[j])
        val = conv(uv_ref, cwv_ref[j], cbv_ref[j])
        act = (_gelu_tanh(g) * val).astype(BF16)
        acc_ref[...] += jnp.dot(act, wd_ref[j], preferred_element_type=F32)
        return carry

    lax.fori_loop(0, D_FF // FFN_COLS, chunk, 0)
    o_ref[0] = acc_ref[...]


def _ffn(h2, x1, wg, wv, cwg, cwv, cbg, cbv, wd):
    b, s, d = x1.shape
    tm, tn = FFN_ROWS, FFN_COLS
    nj = D_FF // tn
    hr = SUBLANES_BF16
    row_blk = pl.BlockSpec((1, tm, d), lambda bi, i: (bi, i, 0))
    halo_blk = pl.BlockSpec((1, hr, d),
                            lambda bi, i: (bi, jnp.maximum(i * (tm // hr) - 1, 0), 0))
    resident = lambda shape: pl.BlockSpec(shape, lambda bi, i: (0,) * len(shape),
                                          pipeline_mode=pl.Buffered(1))
    return pl.pallas_call(
        _ffn_kernel,
        grid=(b, s // tm),
        in_specs=[row_blk, halo_blk, row_blk,
                  resident((nj, d, tn)), resident((nj, d, tn)),
                  resident((nj, FFN_CONV_WIDTH, tn)), resident((nj, FFN_CONV_WIDTH, tn)),
                  resident((nj, 1, tn)), resident((nj, 1, tn)),
                  resident((nj, tn, d))],
        out_specs=row_blk,
        out_shape=jax.ShapeDtypeStruct((b, s, d), F32),
        scratch_shapes=[pltpu.VMEM((hr + tm, tn), F32), pltpu.VMEM((hr + tm, tn), F32),
                        pltpu.VMEM((tm, d), F32)],
        compiler_params=pltpu.CompilerParams(
            dimension_semantics=("arbitrary", "arbitrary"),
            vmem_limit_bytes=VMEM_LIMIT_BYTES),
        name="ffn",
    )(h2, h2, x1, wg, wv, cwg, cwv, cbg, cbv, wd)


def _rope_tables(seq):
    pos = jnp.arange(seq, dtype=F32)
    inv_freq = ROPE_THETA ** (-(jnp.arange(0, ROT_DIM, 2, dtype=F32) / ROT_DIM))
    ang = pos[:, None] * inv_freq[None, :]
    cos, sin = jnp.cos(ang), jnp.sin(ang)
    half = ROT_DIM // 2
    pad = QK_HEAD_DIM - ROT_DIM
    one = jnp.ones((seq, pad), F32)
    zero_h = jnp.zeros((seq, half), F32)
    zero_p = jnp.zeros((seq, pad), F32)
    rc = jnp.concatenate([cos, cos, one], axis=1)
    rs1 = jnp.concatenate([-sin, zero_h, zero_p], axis=1)
    rs2 = jnp.concatenate([zero_h, sin, zero_p], axis=1)
    return tuple(jnp.tile(t, (1, 2)) for t in (rc, rs1, rs2))


def _block_diag(w):
    h, i, j = w.shape
    eye = jnp.eye(h, dtype=w.dtype)
    return (eye[:, None, :, None] * w[:, :, None, :]).reshape(h * i, h * j)


def kernel(x, norm1_w, w_in, q_norm_w, k_norm_w, lambda_q1, lambda_k1, lambda_q2, lambda_k2,
           subln_w, conv_rnn_w, conv_rnn_b, w_gate_a, b_gate_a, w_gate_x, b_gate_x, lru_lambda,
           rnn_norm_w, w_out, norm2_w, w_up, conv_ffn_w, conv_ffn_b, w_down):
    b, s, d = x.shape
    rc, rs1, rs2 = _rope_tables(s)
    nj = D_FF // FFN_COLS
    row = lambda v: v.reshape(1, -1)
    for l in range(DEPTH):
        lambda_init = 0.8 - 0.6 * math.exp(-0.3 * l)
        q, k, v, xr, gr = _in_proj(
            x, row(norm1_w[l]), w_in[l].astype(BF16),
            row(jnp.tile(q_norm_w[l], 2)), row(jnp.tile(k_norm_w[l], 2)), rc, rs1, rs2)
        attn = _attention(q, k, v, row(lambda_q1[l]), row(lambda_k1[l]), row(lambda_q2[l]),
                          row(lambda_k2[l]), row(subln_w[l]), lambda_init)
        y = _rglru(xr, gr, conv_rnn_w[l], row(conv_rnn_b[l]),
                   _block_diag(w_gate_a[l]).astype(BF16), row(b_gate_a[l]),
                   _block_diag(w_gate_x[l]).astype(BF16), row(b_gate_x[l]),
                   row(lru_lambda[l]), row(rnn_norm_w[l]))
        x1, h2 = _out_proj(attn, y, x, w_out[l].astype(BF16), row(norm2_w[l]))
        chunks = lambda w: w.reshape(d, nj, FFN_COLS).transpose(1, 0, 2)
        cchunks = lambda w: w.reshape(-1, nj, FFN_COLS).transpose(1, 0, 2)
        wu = w_up[l].astype(BF16)
        x = _ffn(h2, x1, chunks(wu[:, :D_FF]), chunks(wu[:, D_FF:]),
                 cchunks(conv_ffn_w[l][:, :D_FF]), cchunks(conv_ffn_w[l][:, D_FF:]),
                 cchunks(conv_ffn_b[l][None, :D_FF]), cchunks(conv_ffn_b[l][None, D_FF:]),
                 w_down[l].astype(BF16).reshape(nj, FFN_COLS, d))
    return x
```

```python
import functools
import math

import jax
import jax.numpy as jnp
from jax import lax
from jax.experimental import pallas as pl
from jax.experimental.pallas import tpu as pltpu

D_MODEL = 1024
DEPTH = 1
ATTN_HEADS = 4
QK_HEAD_DIM = 64
V_HEAD_DIM = 2 * QK_HEAD_DIM
ATTN_WIDTH = ATTN_HEADS * V_HEAD_DIM
QK_WIDTH = ATTN_HEADS * 2 * QK_HEAD_DIM
ROT_DIM = QK_HEAD_DIM // 4
ROPE_THETA = 500000.0
RNN_WIDTH = D_MODEL - ATTN_WIDTH
RNN_HEADS = 8
RNN_BLOCK = RNN_WIDTH // RNN_HEADS
RNN_CONV_WIDTH = 4
LRU_C = 8.0
D_FF = 3 * D_MODEL
FFN_CONV_WIDTH = 3
EPS = 1e-6

LANES = 128
SUBLANES_F32 = 8
SUBLANES_BF16 = 16
VMEM_LIMIT_BYTES = 56 * 1024 * 1024

PROJ_ROWS = 512
ATTN_BLOCK = 512
RNN_ROWS = 512
FFN_ROWS = 512
FFN_COLS = 512

LOG2E = math.log2(math.e)
F32 = jnp.float32
BF16 = jnp.bfloat16


def _gelu_tanh(x):
    c = math.sqrt(2.0 / math.pi)
    return 0.5 * x * (1.0 + jnp.tanh(c * (x + 0.044715 * (x * x * x))))


def _in_proj_kernel(x_ref, n1_ref, w_ref, qw_ref, kw_ref, rc_ref, rs1_ref, rs2_ref,
                    q_ref, k_ref, v_ref, xr_ref, gr_ref):
    x = x_ref[0]
    ms = jnp.mean(x * x, axis=-1, keepdims=True)
    h = (x * lax.rsqrt(ms + EPS) * n1_ref[...]).astype(BF16)
    lane = lax.broadcasted_iota(jnp.int32, (1, LANES), 1)
    lo = lane < QK_HEAD_DIM
    rc, rs1, rs2 = rc_ref[...], rs1_ref[...], rs2_ref[...]

    def qk_section(col0, nw_ref, out_ref, out_scale):
        p = jnp.dot(h, w_ref[:, col0:col0 + QK_WIDTH], preferred_element_type=F32)
        nw = nw_ref[...]
        for hd in range(ATTN_HEADS):
            y = p[:, LANES * hd:LANES * (hd + 1)]
            sq = y * y
            s_all = jnp.sum(sq, axis=-1, keepdims=True)
            s_lo = jnp.sum(jnp.where(lo, sq, 0.0), axis=-1, keepdims=True)
            msq = jnp.where(lo, s_lo, s_all - s_lo) * (1.0 / QK_HEAD_DIM)
            y = y * lax.rsqrt(msq + EPS) * nw
            up = pltpu.roll(y, LANES - ROT_DIM // 2, axis=1)
            dn = pltpu.roll(y, ROT_DIM // 2, axis=1)
            y = y * rc + up * rs1 + dn * rs2
            if out_scale != 1.0:
                y = y * out_scale
            out_ref[0, :, LANES * hd:LANES * (hd + 1)] = y.astype(BF16)

    qk_section(0, qw_ref, q_ref, LOG2E * QK_HEAD_DIM ** -0.5)
    qk_section(QK_WIDTH, kw_ref, k_ref, 1.0)
    c = 2 * QK_WIDTH
    v_ref[0] = jnp.dot(h, w_ref[:, c:c + ATTN_WIDTH], preferred_element_type=F32).astype(BF16)
    c += ATTN_WIDTH
    xr_ref[0] = jnp.dot(h, w_ref[:, c:c + RNN_WIDTH], preferred_element_type=F32)
    c += RNN_WIDTH
    gr_ref[0] = jnp.dot(h, w_ref[:, c:c + RNN_WIDTH], preferred_element_type=F32)


def _in_proj(x, n1, w_in, qw2, kw2, rc, rs1, rs2):
    b, s, d = x.shape
    tm = PROJ_ROWS
    n_out = w_in.shape[1]
    row_blk = lambda w: pl.BlockSpec((1, tm, w), lambda bi, i: (bi, i, 0))
    const = lambda shape: pl.BlockSpec(shape, lambda bi, i: (0,) * len(shape))
    rope_blk = pl.BlockSpec((tm, LANES), lambda bi, i: (i, 0))
    return pl.pallas_call(
        _in_proj_kernel,
        grid=(b, s // tm),
        in_specs=[row_blk(d), const((1, d)), const((d, n_out)), const((1, LANES)),
                  const((1, LANES)), rope_blk, rope_blk, rope_blk],
        out_specs=[row_blk(QK_WIDTH), row_blk(QK_WIDTH), row_blk(ATTN_WIDTH),
                   row_blk(RNN_WIDTH), row_blk(RNN_WIDTH)],
        out_shape=[jax.ShapeDtypeStruct((b, s, QK_WIDTH), BF16),
                   jax.ShapeDtypeStruct((b, s, QK_WIDTH), BF16),
                   jax.ShapeDtypeStruct((b, s, ATTN_WIDTH), BF16),
                   jax.ShapeDtypeStruct((b, s, RNN_WIDTH), F32),
                   jax.ShapeDtypeStruct((b, s, RNN_WIDTH), F32)],
        compiler_params=pltpu.CompilerParams(
            dimension_semantics=("arbitrary", "arbitrary"),
            vmem_limit_bytes=VMEM_LIMIT_BYTES),
        name="in_proj",
    )(x, n1, w_in, qw2, kw2, rc, rs1, rs2)


def _attn_kernel(q_ref, kt_ref, v_ref, lq1_ref, lk1_ref, lq2_ref, lk2_ref, sw_ref, o_ref,
                 m_ref, l_ref, acc_ref, *, lambda_init):
    t = ATTN_BLOCK
    nl = t // LANES
    qi = pl.program_id(2)
    q = q_ref[0]
    lane = lax.broadcasted_iota(jnp.int32, (1, LANES), 1)
    zero = jnp.zeros_like(q)
    qz = (jnp.where(lane < QK_HEAD_DIM, q, zero), jnp.where(lane >= QK_HEAD_DIM, q, zero))

    m_ref[...] = jnp.full(m_ref.shape, -1e30, F32)
    l_ref[...] = jnp.zeros(l_ref.shape, F32)
    acc_ref[...] = jnp.zeros(acc_ref.shape, F32)

    def step(j, masked):
        c0 = pl.multiple_of(j * t, t)
        kt = kt_ref[0, :, pl.ds(c0, t)]
        vb = v_ref[0, pl.ds(c0, t), :]
        for c in range(2):
            s = jnp.dot(qz[c], kt, preferred_element_type=F32)
            if masked:
                row = lax.broadcasted_iota(jnp.int32, (t, t), 0)
                col = lax.broadcasted_iota(jnp.int32, (t, t), 1)
                s = jnp.where(col <= row, s, -jnp.inf)
            m_prev = m_ref[c]
            m_new = jnp.maximum(m_prev, jnp.max(s, axis=-1, keepdims=True))
            alpha = jnp.exp2(m_prev - m_new)
            es = [jnp.exp2(s[:, LANES * k:LANES * (k + 1)] - m_new) for k in range(nl)]
            part = es[0]
            for k in range(1, nl):
                part = part + es[k]
            l_ref[c] = alpha * l_ref[c] + part
            e = jnp.concatenate([x.astype(BF16) for x in es], axis=1)
            acc_ref[c] = alpha * acc_ref[c] + jnp.dot(e, vb, preferred_element_type=F32)
            m_ref[c] = m_new

    def full_step(j, carry):
        step(j, False)
        return carry

    lax.fori_loop(0, qi, full_step, 0)
    step(qi, True)

    lam = (jnp.exp(jnp.sum(lq1_ref[...] * lk1_ref[...], axis=-1, keepdims=True))
           - jnp.exp(jnp.sum(lq2_ref[...] * lk2_ref[...], axis=-1, keepdims=True))
           + lambda_init)
    l0 = jnp.sum(l_ref[0], axis=-1, keepdims=True)
    l1 = jnp.sum(l_ref[1], axis=-1, keepdims=True)
    o = acc_ref[0] / l0 - lam * (acc_ref[1] / l1)
    ms = jnp.mean(o * o, axis=-1, keepdims=True)
    o = o * lax.rsqrt(ms + EPS) * sw_ref[...] * (1.0 - lambda_init)
    o_ref[0] = o.astype(BF16)


def _attention(q, kt, v, lq1, lk1, lq2, lk2, sw, lambda_init):
    b, s, _ = q.shape
    t = ATTN_BLOCK
    vec = lambda n: pl.BlockSpec((1, n), lambda bi, h, i: (0, 0))
    return pl.pallas_call(
        functools.partial(_attn_kernel, lambda_init=lambda_init),
        grid=(b, ATTN_HEADS, s // t),
        in_specs=[pl.BlockSpec((1, t, LANES), lambda bi, h, i: (bi, i, h)),
                  pl.BlockSpec((1, LANES, s), lambda bi, h, i: (bi, h, 0)),
                  pl.BlockSpec((1, s, LANES), lambda bi, h, i: (bi, 0, h)),
                  vec(QK_HEAD_DIM), vec(QK_HEAD_DIM), vec(QK_HEAD_DIM), vec(QK_HEAD_DIM),
                  vec(V_HEAD_DIM)],
        out_specs=pl.BlockSpec((1, t, LANES), lambda bi, h, i: (bi, i, h)),
        out_shape=jax.ShapeDtypeStruct((b, s, ATTN_WIDTH), BF16),
        scratch_shapes=[pltpu.VMEM((2, t, LANES), F32), pltpu.VMEM((2, t, LANES), F32),
                        pltpu.VMEM((2, t, V_HEAD_DIM), F32)],
        compiler_params=pltpu.CompilerParams(
            dimension_semantics=("arbitrary", "arbitrary", "arbitrary"),
            vmem_limit_bytes=VMEM_LIMIT_BYTES),
        name="diff_attn",
    )(q, kt, v, lq1, lk1, lq2, lk2, sw)


def _rglru_kernel(xr_ref, gr_ref, cw_ref, cb_ref, wa_ref, ba_ref, wx_ref, bx_ref, lam_ref,
                  nw_ref, y_ref, ext_ref, a_ref, b_ref, hc_ref):
    ts = RNN_ROWS
    halo = SUBLANES_F32

    @pl.when(pl.program_id(1) == 0)
    def _():
        ext_ref[0:halo, :] = jnp.zeros((halo, RNN_WIDTH), F32)
        hc_ref[...] = jnp.zeros((halo, RNN_WIDTH), F32)

    x = xr_ref[0]
    ext_ref[halo:halo + ts, :] = x
    xc = cb_ref[...] + cw_ref[RNN_CONV_WIDTH - 1:RNN_CONV_WIDTH, :] * x
    for d in range(1, RNN_CONV_WIDTH):
        tap = RNN_CONV_WIDTH - 1 - d
        xc = xc + cw_ref[tap:tap + 1, :] * ext_ref[halo - d:halo - d + ts, :]
    ext_ref[0:halo, :] = x[ts - halo:ts, :]

    xb = xc.astype(BF16)
    r = jax.nn.sigmoid(jnp.dot(xb, wa_ref[...], preferred_element_type=F32) + ba_ref[...])
    ig = jax.nn.sigmoid(jnp.dot(xb, wx_ref[...], preferred_element_type=F32) + bx_ref[...])
    z = -lam_ref[...]
    softplus = jnp.maximum(z, 0.0) + jnp.log1p(jnp.exp(-jnp.abs(z)))
    log_a = -LRU_C * r * softplus
    a = jnp.exp(log_a)
    u = jnp.sqrt(1.0 - a * a) * ig * xc

    sub = lax.broadcasted_iota(jnp.int32, (ts, 1), 0) % SUBLANES_F32
    d = 1
    while d < SUBLANES_F32:
        keep = sub >= d
        a_prev = jnp.where(keep, pltpu.roll(a, d, axis=0), 1.0)
        u_prev = jnp.where(keep, pltpu.roll(u, d, axis=0), 0.0)
        u = a * u_prev + u
        a = a * a_prev
        d *= 2
    a_ref[...] = a
    b_ref[...] = u

    def tile(j, h_prev):
        r0 = pl.multiple_of(j * SUBLANES_F32, SUBLANES_F32)
        h = b_ref[pl.ds(r0, SUBLANES_F32), :] + a_ref[pl.ds(r0, SUBLANES_F32), :] * h_prev
        b_ref[pl.ds(r0, SUBLANES_F32), :] = h
        return jnp.broadcast_to(h[SUBLANES_F32 - 1:SUBLANES_F32, :], h.shape)

    hc_ref[...] = lax.fori_loop(0, ts // SUBLANES_F32, tile, hc_ref[...], unroll=8)

    y = b_ref[...] * _gelu_tanh(gr_ref[0])
    ms = jnp.mean(y * y, axis=-1, keepdims=True)
    y_ref[0] = (y * lax.rsqrt(ms + EPS) * nw_ref[...]).astype(BF16)


def _rglru(xr, gr, cw, cb, wa, ba, wx, bx, lam, nw):
    b, s, c = xr.shape
    ts = RNN_ROWS
    row_blk = pl.BlockSpec((1, ts, c), lambda bi, i: (bi, i, 0))
    const = lambda shape: pl.BlockSpec(shape, lambda bi, i: (0,) * len(shape))
    return pl.pallas_call(
        _rglru_kernel,
        grid=(b, s // ts),
        in_specs=[row_blk, row_blk, const((RNN_CONV_WIDTH, c)), const((1, c)),
                  const((c, c)), const((1, c)), const((c, c)), const((1, c)),
                  const((1, c)), const((1, c))],
        out_specs=row_blk,
        out_shape=jax.ShapeDtypeStruct((b, s, c), BF16),
        scratch_shapes=[pltpu.VMEM((ts + SUBLANES_F32, c), F32),
                        pltpu.VMEM((ts, c), F32), pltpu.VMEM((ts, c), F32),
                        pltpu.VMEM((SUBLANES_F32, c), F32)],
        compiler_params=pltpu.CompilerParams(
            dimension_semantics=("arbitrary", "arbitrary"),
            vmem_limit_bytes=VMEM_LIMIT_BYTES),
        name="rglru",
    )(xr, gr, cw, cb, wa, ba, wx, bx, lam, nw)


def _out_proj_kernel(attn_ref, y_ref, x_ref, w_ref, n2_ref, x1_ref, h2_ref):
    o = jnp.dot(attn_ref[0], w_ref[0:ATTN_WIDTH, :], preferred_element_type=F32)
    o = o + jnp.dot(y_ref[0], w_ref[ATTN_WIDTH:, :], preferred_element_type=F32)
    x1 = x_ref[0] + o
    x1_ref[0] = x1
    ms = jnp.mean(x1 * x1, axis=-1, keepdims=True)
    h2_ref[0] = (x1 * lax.rsqrt(ms + EPS) * n2_ref[...]).astype(BF16)


def _out_proj(attn, y, x, w_out, n2):
    b, s, d = x.shape
    tm = PROJ_ROWS
    row_blk = lambda w: pl.BlockSpec((1, tm, w), lambda bi, i: (bi, i, 0))
    const = lambda shape: pl.BlockSpec(shape, lambda bi, i: (0,) * len(shape))
    return pl.pallas_call(
        _out_proj_kernel,
        grid=(b, s // tm),
        in_specs=[row_blk(ATTN_WIDTH), row_blk(RNN_WIDTH), row_blk(d), const((d, d)),
                  const((1, d))],
        out_specs=[row_blk(d), row_blk(d)],
        out_shape=[jax.ShapeDtypeStruct((b, s, d), F32), jax.ShapeDtypeStruct((b, s, d), BF16)],
        compiler_params=pltpu.CompilerParams(
            dimension_semantics=("arbitrary", "arbitrary"),
            vmem_limit_bytes=VMEM_LIMIT_BYTES),
        name="out_proj",
    )(attn, y, x, w_out, n2)


def _ffn_kernel(h2_ref, halo_ref, x1_ref, wg_ref, wv_ref, cwg_ref, cwv_ref, cbg_ref, cbv_ref,
                wd_ref, o_ref, ug_ref, uv_ref, acc_ref):
    tm = FFN_ROWS
    hr = SUBLANES_BF16
    halo = halo_ref[0]
    halo = jnp.where(pl.program_id(1) > 0, halo, jnp.zeros_like(halo))
    hext = jnp.concatenate([halo, h2_ref[0]], axis=0)
    acc_ref[...] = x1_ref[0]

    def conv(u_ref, cw, cb):
        y = cb + cw[FFN_CONV_WIDTH - 1:FFN_CONV_WIDTH, :] * u_ref[hr:hr + tm, :]
        for d in range(1, FFN_CONV_WIDTH):
            tap = FFN_CONV_WIDTH - 1 - d
            y = y + cw[tap:tap + 1, :] * u_ref[hr - d:hr - d + tm, :]
        return y

    def chunk(j, carry):
        ug_ref[...] = jnp.dot(hext, wg_ref[j], preferred_element_type=F32)
        uv_ref[...] = jnp.dot(hext, wv_ref[j], preferred_element_type=F32)
        g = conv(ug_ref, cwg_ref[j], cbg_ref[j])
        val = conv(uv_ref, cwv_ref[j], cbv_ref[j])
        act = (_gelu_tanh(g) * val).astype(BF16)
        acc_ref[...] += jnp.dot(act, wd_ref[j], preferred_element_type=F32)
        return carry

    lax.fori_loop(0, D_FF // FFN_COLS, chunk, 0)
    o_ref[0] = acc_ref[...]


def _ffn(h2, x1, wg, wv, cwg, cwv, cbg, cbv, wd):
    b, s, d = x1.shape
    tm, tn = FFN_ROWS, FFN_COLS
    nj = D_FF // tn
    hr = SUBLANES_BF16
    row_blk = pl.BlockSpec((1, tm, d), lambda bi, i: (bi, i, 0))
    halo_blk = pl.BlockSpec((1, hr, d),
                            lambda bi, i: (bi, jnp.maximum(i * (tm // hr) - 1, 0), 0))
    resident = lambda shape: pl.BlockSpec(shape, lambda bi, i: (0,) * len(shape),
                                          pipeline_mode=pl.Buffered(1))
    return pl.pallas_call(
        _ffn_kernel,
        grid=(b, s // tm),
        in_specs=[row_blk, halo_blk, row_blk,
                  resident((nj, d, tn)), resident((nj, d, tn)),
                  resident((nj, FFN_CONV_WIDTH, tn)), resident((nj, FFN_CONV_WIDTH, tn)),
                  resident((nj, 1, tn)), resident((nj, 1, tn)),
                  resident((nj, tn, d))],
        out_specs=row_blk,
        out_shape=jax.ShapeDtypeStruct((b, s, d), F32),
        scratch_shapes=[pltpu.VMEM((hr + tm, tn), F32), pltpu.VMEM((hr + tm, tn), F32),
                        pltpu.VMEM((tm, d), F32)],
        compiler_params=pltpu.CompilerParams(
            dimension_semantics=("arbitrary", "arbitrary"),
            vmem_limit_bytes=VMEM_LIMIT_BYTES),
        name="ffn",
    )(h2, h2, x1, wg, wv, cwg, cwv, cbg, cbv, wd)


def _rope_tables(seq):
    pos = jnp.arange(seq, dtype=F32)
    inv_freq = ROPE_THETA ** (-(jnp.arange(0, ROT_DIM, 2, dtype=F32) / ROT_DIM))
    ang = pos[:, None] * inv_freq[None, :]
    cos, sin = jnp.cos(ang), jnp.sin(ang)
    half = ROT_DIM // 2
    pad = QK_HEAD_DIM - ROT_DIM
    one = jnp.ones((seq, pad), F32)
    zero_h = jnp.zeros((seq, half), F32)
    zero_p = jnp.zeros((seq, pad), F32)
    rc = jnp.concatenate([cos, cos, one], axis=1)
    rs1 = jnp.concatenate([-sin, zero_h, zero_p], axis=1)
    rs2 = jnp.concatenate([zero_h, sin, zero_p], axis=1)
    return tuple(jnp.tile(t, (1, 2)) for t in (rc, rs1, rs2))


def _block_diag(w):
    h, i, j = w.shape
    eye = jnp.eye(h, dtype=w.dtype)
    return (eye[:, None, :, None] * w[:, :, None, :]).reshape(h * i, h * j)


def kernel(x, norm1_w, w_in, q_norm_w, k_norm_w, lambda_q1, lambda_k1, lambda_q2, lambda_k2,
           subln_w, conv_rnn_w, conv_rnn_b, w_gate_a, b_gate_a, w_gate_x, b_gate_x, lru_lambda,
           rnn_norm_w, w_out, norm2_w, w_up, conv_ffn_w, conv_ffn_b, w_down):
    b, s, d = x.shape
    rc, rs1, rs2 = _rope_tables(s)
    nj = D_FF // FFN_COLS
    row = lambda v: v.reshape(1, -1)
    for l in range(DEPTH):
        lambda_init = 0.8 - 0.6 * math.exp(-0.3 * l)
        q, k, v, xr, gr = _in_proj(
            x, row(norm1_w[l]), w_in[l].astype(BF16),
            row(jnp.tile(q_norm_w[l], 2)), row(jnp.tile(k_norm_w[l], 2)), rc, rs1, rs2)
        attn = _attention(q, k.transpose(0, 2, 1), v, row(lambda_q1[l]), row(lambda_k1[l]), row(lambda_q2[l]),
                          row(lambda_k2[l]), row(subln_w[l]), lambda_init)
        y = _rglru(xr, gr, conv_rnn_w[l], row(conv_rnn_b[l]),
                   _block_diag(w_gate_a[l]).astype(BF16), row(b_gate_a[l]),
                   _block_diag(w_gate_x[l]).astype(BF16), row(b_gate_x[l]),
                   row(lru_lambda[l]), row(rnn_norm_w[l]))
        x1, h2 = _out_proj(attn, y, x, w_out[l].astype(BF16), row(norm2_w[l]))
        chunks = lambda w: w.reshape(d, nj, FFN_COLS).transpose(1, 0, 2)
        cchunks = lambda w: w.reshape(-1, nj, FFN_COLS).transpose(1, 0, 2)
        wu = w_up[l].astype(BF16)
        x = _ffn(h2, x1, chunks(wu[:, :D_FF]), chunks(wu[:, D_FF:]),
                 cchunks(conv_ffn_w[l][:, :D_FF]), cchunks(conv_ffn_w[l][:, D_FF:]),
                 cchunks(conv_ffn_b[l][None, :D_FF]), cchunks(conv_ffn_b[l][None, D_FF:]),
                 w_down[l].astype(BF16).reshape(nj, FFN_COLS, d))
    return x
```

```python
import functools
import math

import jax
import jax.numpy as jnp
from jax import lax
from jax.experimental import pallas as pl
from jax.experimental.pallas import tpu as pltpu

D_MODEL = 1024
DEPTH = 1
ATTN_HEADS = 4
QK_HEAD_DIM = 64
V_HEAD_DIM = 2 * QK_HEAD_DIM
ATTN_WIDTH = ATTN_HEADS * V_HEAD_DIM
QK_WIDTH = ATTN_HEADS * 2 * QK_HEAD_DIM
ROT_DIM = QK_HEAD_DIM // 4
ROPE_THETA = 500000.0
RNN_WIDTH = D_MODEL - ATTN_WIDTH
RNN_HEADS = 8
RNN_BLOCK = RNN_WIDTH // RNN_HEADS
RNN_CONV_WIDTH = 4
LRU_C = 8.0
D_FF = 3 * D_MODEL
FFN_CONV_WIDTH = 3
EPS = 1e-6

LANES = 128
SUBLANES_F32 = 8
SUBLANES_BF16 = 16
VMEM_LIMIT_BYTES = 56 * 1024 * 1024

PROJ_ROWS = 512
ATTN_BLOCK = 512
RNN_ROWS = 512
FFN_ROWS = 512
FFN_COLS = 512

LOG2E = math.log2(math.e)
F32 = jnp.float32
BF16 = jnp.bfloat16


def _gelu_tanh(x):
    c = math.sqrt(2.0 / math.pi)
    return 0.5 * x * (1.0 + jnp.tanh(c * (x + 0.044715 * (x * x * x))))


def _in_proj_kernel(x_ref, n1_ref, w_ref, qw_ref, kw_ref, rc_ref, rs1_ref, rs2_ref,
                    qt_ref, k_ref, vt_ref, xr_ref, gr_ref):
    x = x_ref[0]
    ms = jnp.mean(x * x, axis=-1, keepdims=True)
    h = (x * lax.rsqrt(ms + EPS) * n1_ref[...]).astype(BF16)
    lane = lax.broadcasted_iota(jnp.int32, (1, LANES), 1)
    lo = lane < QK_HEAD_DIM
    rc, rs1, rs2 = rc_ref[...], rs1_ref[...], rs2_ref[...]

    def qk_section(col0, nw_ref, out_ref, out_scale, transposed):
        p = jnp.dot(h, w_ref[:, col0:col0 + QK_WIDTH], preferred_element_type=F32)
        nw = nw_ref[...]
        for hd in range(ATTN_HEADS):
            y = p[:, LANES * hd:LANES * (hd + 1)]
            sq = y * y
            s_all = jnp.sum(sq, axis=-1, keepdims=True)
            s_lo = jnp.sum(jnp.where(lo, sq, 0.0), axis=-1, keepdims=True)
            msq = jnp.where(lo, s_lo, s_all - s_lo) * (1.0 / QK_HEAD_DIM)
            y = y * lax.rsqrt(msq + EPS) * nw
            up = pltpu.roll(y, LANES - ROT_DIM // 2, axis=1)
            dn = pltpu.roll(y, ROT_DIM // 2, axis=1)
            y = y * rc + up * rs1 + dn * rs2
            if out_scale != 1.0:
                y = y * out_scale
            if transposed:
                out_ref[0, LANES * hd:LANES * (hd + 1), :] = y.T.astype(BF16)
            else:
                out_ref[0, :, LANES * hd:LANES * (hd + 1)] = y.astype(BF16)

    qk_section(0, qw_ref, qt_ref, LOG2E * QK_HEAD_DIM ** -0.5, True)
    qk_section(QK_WIDTH, kw_ref, k_ref, 1.0, False)
    c = 2 * QK_WIDTH
    pv = jnp.dot(h, w_ref[:, c:c + ATTN_WIDTH], preferred_element_type=F32)
    for hd in range(ATTN_HEADS):
        vt_ref[0, LANES * hd:LANES * (hd + 1), :] = (
            pv[:, LANES * hd:LANES * (hd + 1)].T.astype(BF16))
    c += ATTN_WIDTH
    xr_ref[0] = jnp.dot(h, w_ref[:, c:c + RNN_WIDTH], preferred_element_type=F32)
    c += RNN_WIDTH
    gr_ref[0] = jnp.dot(h, w_ref[:, c:c + RNN_WIDTH], preferred_element_type=F32)


def _in_proj(x, n1, w_in, qw2, kw2, rc, rs1, rs2):
    b, s, d = x.shape
    tm = PROJ_ROWS
    n_out = w_in.shape[1]
    row_blk = lambda w: pl.BlockSpec((1, tm, w), lambda bi, i: (bi, i, 0))
    col_blk = lambda w: pl.BlockSpec((1, w, tm), lambda bi, i: (bi, 0, i))
    const = lambda shape: pl.BlockSpec(shape, lambda bi, i: (0,) * len(shape))
    rope_blk = pl.BlockSpec((tm, LANES), lambda bi, i: (i, 0))
    return pl.pallas_call(
        _in_proj_kernel,
        grid=(b, s // tm),
        in_specs=[row_blk(d), const((1, d)), const((d, n_out)), const((1, LANES)),
                  const((1, LANES)), rope_blk, rope_blk, rope_blk],
        out_specs=[col_blk(QK_WIDTH), row_blk(QK_WIDTH), col_blk(ATTN_WIDTH),
                   row_blk(RNN_WIDTH), row_blk(RNN_WIDTH)],
        out_shape=[jax.ShapeDtypeStruct((b, QK_WIDTH, s), BF16),
                   jax.ShapeDtypeStruct((b, s, QK_WIDTH), BF16),
                   jax.ShapeDtypeStruct((b, ATTN_WIDTH, s), BF16),
                   jax.ShapeDtypeStruct((b, s, RNN_WIDTH), F32),
                   jax.ShapeDtypeStruct((b, s, RNN_WIDTH), F32)],
        compiler_params=pltpu.CompilerParams(
            dimension_semantics=("arbitrary", "arbitrary"),
            vmem_limit_bytes=VMEM_LIMIT_BYTES),
        name="in_proj",
    )(x, n1, w_in, qw2, kw2, rc, rs1, rs2)


def _attn_kernel(qt_ref, k_ref, vt_ref, lq1_ref, lk1_ref, lq2_ref, lk2_ref, sw_ref, o_ref,
                 m_ref, l_ref, acc_ref, sa_ref, mxa_ref, sb_ref, mxb_ref, *, lambda_init):
    t = ATTN_BLOCK
    qi = pl.program_id(2)
    qt = qt_ref[0]
    dim = lax.broadcasted_iota(jnp.int32, (LANES, 1), 0)
    zero = jnp.zeros_like(qt)
    qz = (jnp.where(dim < QK_HEAD_DIM, qt, zero), jnp.where(dim >= QK_HEAD_DIM, qt, zero))

    m_ref[...] = jnp.full(m_ref.shape, -1e30, F32)
    l_ref[...] = jnp.zeros(l_ref.shape, F32)
    acc_ref[...] = jnp.zeros(acc_ref.shape, F32)

    def scores(buf, j, masked):
        s_ref, mx_ref = buf
        r0 = pl.multiple_of(j * t, t)
        kb = k_ref[0, pl.ds(r0, t), :]
        for c in range(2):
            s = jnp.dot(kb, qz[c], preferred_element_type=F32)
            if masked:
                key = lax.broadcasted_iota(jnp.int32, (t, t), 0)
                qry = lax.broadcasted_iota(jnp.int32, (t, t), 1)
                s = jnp.where(key <= qry, s, -jnp.inf)
            s_ref[c] = s
            mx_ref[c] = jnp.max(s, axis=0, keepdims=True)

    def accumulate(buf, j):
        s_ref, mx_ref = buf
        r0 = pl.multiple_of(j * t, t)
        vtb = vt_ref[0, :, pl.ds(r0, t)]
        for c in range(2):
            m_prev = m_ref[c]
            m_new = jnp.maximum(m_prev, mx_ref[c])
            alpha = jnp.exp2(m_prev - m_new)
            e = jnp.exp2(s_ref[c] - m_new)
            l_ref[c] = alpha * l_ref[c] + jnp.sum(e, axis=0, keepdims=True)
            acc_ref[c] = alpha * acc_ref[c] + jnp.dot(vtb, e.astype(BF16),
                                                      preferred_element_type=F32)
            m_ref[c] = m_new

    buf_a, buf_b = (sa_ref, mxa_ref), (sb_ref, mxb_ref)

    @pl.when(qi > 0)
    def _():
        scores(buf_a, 0, False)

    def pair(i, carry):
        scores(buf_b, 2 * i + 1, False)
        accumulate(buf_a, 2 * i)
        scores(buf_a, 2 * i + 2, False)
        accumulate(buf_b, 2 * i + 1)
        return carry

    n_pairs = lax.shift_right_logical(jnp.maximum(qi - 1, 0), 1)
    lax.fori_loop(0, n_pairs, pair, 0)
    odd = (qi & 1) == 1

    @pl.when(qi == 0)
    def _():
        scores(buf_a, 0, True)
        accumulate(buf_a, 0)

    @pl.when(odd)
    def _():
        scores(buf_b, qi, True)
        accumulate(buf_a, qi - 1)
        accumulate(buf_b, qi)

    @pl.when(jnp.logical_and(qi > 0, jnp.logical_not(odd)))
    def _():
        scores(buf_b, qi - 1, False)
        accumulate(buf_a, qi - 2)
        scores(buf_a, qi, True)
        accumulate(buf_b, qi - 1)
        accumulate(buf_a, qi)
    lam = (jnp.exp(jnp.sum(lq1_ref[...] * lk1_ref[...], axis=-1, keepdims=True))
           - jnp.exp(jnp.sum(lq2_ref[...] * lk2_ref[...], axis=-1, keepdims=True))
           + lambda_init)
    o = acc_ref[0] * (1.0 / l_ref[0]) - acc_ref[1] * (lam / l_ref[1])
    ms = jnp.mean(o * o, axis=0, keepdims=True)
    o = o * (lax.rsqrt(ms + EPS) * (1.0 - lambda_init))
    o_ref[0] = (o.T * sw_ref[...]).astype(BF16)


def _attention(qt, k, vt, lq1, lk1, lq2, lk2, sw, lambda_init):
    b, s, _ = k.shape
    t = ATTN_BLOCK
    vec = lambda n: pl.BlockSpec((1, n), lambda bi, h, i: (0, 0))
    return pl.pallas_call(
        functools.partial(_attn_kernel, lambda_init=lambda_init),
        grid=(b, ATTN_HEADS, s // t),
        in_specs=[pl.BlockSpec((1, LANES, t), lambda bi, h, i: (bi, h, i)),
                  pl.BlockSpec((1, s, LANES), lambda bi, h, i: (bi, 0, h)),
                  pl.BlockSpec((1, LANES, s), lambda bi, h, i: (bi, h, 0)),
                  vec(QK_HEAD_DIM), vec(QK_HEAD_DIM), vec(QK_HEAD_DIM), vec(QK_HEAD_DIM),
                  vec(V_HEAD_DIM)],
        out_specs=pl.BlockSpec((1, t, LANES), lambda bi, h, i: (bi, i, h)),
        out_shape=jax.ShapeDtypeStruct((b, s, ATTN_WIDTH), BF16),
        scratch_shapes=[pltpu.VMEM((2, 1, t), F32), pltpu.VMEM((2, 1, t), F32),
                        pltpu.VMEM((2, V_HEAD_DIM, t), F32),
                        pltpu.VMEM((2, t, t), F32), pltpu.VMEM((2, 1, t), F32),
                        pltpu.VMEM((2, t, t), F32), pltpu.VMEM((2, 1, t), F32)],
        compiler_params=pltpu.CompilerParams(
            dimension_semantics=("arbitrary", "arbitrary", "arbitrary"),
            vmem_limit_bytes=VMEM_LIMIT_BYTES),
        name="diff_attn",
    )(qt, k, vt, lq1, lk1, lq2, lk2, sw)


def _rglru_kernel(xr_ref, gr_ref, cw_ref, cb_ref, wa_ref, ba_ref, wx_ref, bx_ref, lam_ref,
                  nw_ref, y_ref, ext_ref, a_ref, b_ref, hc_ref):
    ts = RNN_ROWS
    halo = SUBLANES_F32

    @pl.when(pl.program_id(1) == 0)
    def _():
        ext_ref[0:halo, :] = jnp.zeros((halo, RNN_WIDTH), F32)
        hc_ref[...] = jnp.zeros((halo, RNN_WIDTH), F32)

    x = xr_ref[0]
    ext_ref[halo:halo + ts, :] = x
    xc = cb_ref[...] + cw_ref[RNN_CONV_WIDTH - 1:RNN_CONV_WIDTH, :] * x
    for d in range(1, RNN_CONV_WIDTH):
        tap = RNN_CONV_WIDTH - 1 - d
        xc = xc + cw_ref[tap:tap + 1, :] * ext_ref[halo - d:halo - d + ts, :]
    ext_ref[0:halo, :] = x[ts - halo:ts, :]

    xb = xc.astype(BF16)
    r = jax.nn.sigmoid(jnp.dot(xb, wa_ref[...], preferred_element_type=F32) + ba_ref[...])
    ig = jax.nn.sigmoid(jnp.dot(xb, wx_ref[...], preferred_element_type=F32) + bx_ref[...])
    z = -lam_ref[...]
    softplus = jnp.maximum(z, 0.0) + jnp.log1p(jnp.exp(-jnp.abs(z)))
    log_a = -LRU_C * r * softplus
    a = jnp.exp(log_a)
    w = 1.0 - a * a
    mult = jnp.where(w > 0.0, w * lax.rsqrt(w), 0.0)
    u = mult * ig * xc

    tiles = (ts // SUBLANES_F32, SUBLANES_F32, RNN_WIDTH)
    a = a.reshape(tiles)
    u = u.reshape(tiles)
    sub = lax.broadcasted_iota(jnp.int32, (1, SUBLANES_F32, 1), 1)
    d = 1
    while d < SUBLANES_F32:
        keep = sub >= d
        a_prev = jnp.where(keep, pltpu.roll(a, d, axis=1), 1.0)
        u_prev = jnp.where(keep, pltpu.roll(u, d, axis=1), 0.0)
        u = a * u_prev + u
        a = a * a_prev
        d *= 2
    a_ref[...] = a.reshape(ts, RNN_WIDTH)
    b_ref[...] = u.reshape(ts, RNN_WIDTH)

    def tile(j, h_prev):
        r0 = pl.multiple_of(j * SUBLANES_F32, SUBLANES_F32)
        h = b_ref[pl.ds(r0, SUBLANES_F32), :] + a_ref[pl.ds(r0, SUBLANES_F32), :] * h_prev
        b_ref[pl.ds(r0, SUBLANES_F32), :] = h
        return jnp.broadcast_to(h[SUBLANES_F32 - 1:SUBLANES_F32, :], h.shape)

    hc_ref[...] = lax.fori_loop(0, ts // SUBLANES_F32, tile, hc_ref[...], unroll=8)

    y = b_ref[...] * _gelu_tanh(gr_ref[0])
    ms = jnp.mean(y * y, axis=-1, keepdims=True)
    y_ref[0] = (y * lax.rsqrt(ms + EPS) * nw_ref[...]).astype(BF16)


def _rglru(xr, gr, cw, cb, wa, ba, wx, bx, lam, nw):
    b, s, c = xr.shape
    ts = RNN_ROWS
    row_blk = pl.BlockSpec((1, ts, c), lambda bi, i: (bi, i, 0))
    const = lambda shape: pl.BlockSpec(shape, lambda bi, i: (0,) * len(shape))
    return pl.pallas_call(
        _rglru_kernel,
        grid=(b, s // ts),
        in_specs=[row_blk, row_blk, const((RNN_CONV_WIDTH, c)), const((1, c)),
                  const((c, c)), const((1, c)), const((c, c)), const((1, c)),
                  const((1, c)), const((1, c))],
        out_specs=row_blk,
        out_shape=jax.ShapeDtypeStruct((b, s, c), BF16),
        scratch_shapes=[pltpu.VMEM((ts + SUBLANES_F32, c), F32),
                        pltpu.VMEM((ts, c), F32), pltpu.VMEM((ts, c), F32),
                        pltpu.VMEM((SUBLANES_F32, c), F32)],
        compiler_params=pltpu.CompilerParams(
            dimension_semantics=("arbitrary", "arbitrary"),
            vmem_limit_bytes=VMEM_LIMIT_BYTES),
        name="rglru",
    )(xr, gr, cw, cb, wa, ba, wx, bx, lam, nw)


def _out_proj_kernel(attn_ref, y_ref, x_ref, w_ref, n2_ref, x1_ref, h2_ref):
    o = jnp.dot(attn_ref[0], w_ref[0:ATTN_WIDTH, :], preferred_element_type=F32)
    o = o + jnp.dot(y_ref[0], w_ref[ATTN_WIDTH:, :], preferred_element_type=F32)
    x1 = x_ref[0] + o
    x1_ref[0] = x1
    ms = jnp.mean(x1 * x1, axis=-1, keepdims=True)
    h2_ref[0] = (x1 * lax.rsqrt(ms + EPS) * n2_ref[...]).astype(BF16)


def _out_proj(attn, y, x, w_out, n2):
    b, s, d = x.shape
    tm = PROJ_ROWS
    row_blk = lambda w: pl.BlockSpec((1, tm, w), lambda bi, i: (bi, i, 0))
    const = lambda shape: pl.BlockSpec(shape, lambda bi, i: (0,) * len(shape))
    return pl.pallas_call(
        _out_proj_kernel,
        grid=(b, s // tm),
        in_specs=[row_blk(ATTN_WIDTH), row_blk(RNN_WIDTH), row_blk(d), const((d, d)),
                  const((1, d))],
        out_specs=[row_blk(d), row_blk(d)],
        out_shape=[jax.ShapeDtypeStruct((b, s, d), F32), jax.ShapeDtypeStruct((b, s, d), BF16)],
        compiler_params=pltpu.CompilerParams(
            dimension_semantics=("arbitrary", "arbitrary"),
            vmem_limit_bytes=VMEM_LIMIT_BYTES),
        name="out_proj",
    )(attn, y, x, w_out, n2)


def _ffn_kernel(h2_ref, halo_ref, x1_ref, wg_ref, wv_ref, cwg_ref, cwv_ref, cbg_ref, cbv_ref,
                wd_ref, o_ref, uga_ref, uva_ref, ugb_ref, uvb_ref):
    tm = FFN_ROWS
    hr = SUBLANES_BF16
    nj = D_FF // FFN_COLS
    halo = halo_ref[0]
    halo = jnp.where(pl.program_id(1) > 0, halo, jnp.zeros_like(halo))
    hext = jnp.concatenate([halo, h2_ref[0]], axis=0)
    o_ref[0] = x1_ref[0]

    def conv(u_ref, cw, cb):
        y = cb + cw[FFN_CONV_WIDTH - 1:FFN_CONV_WIDTH, :] * u_ref[hr:hr + tm, :]
        for d in range(1, FFN_CONV_WIDTH):
            tap = FFN_CONV_WIDTH - 1 - d
            y = y + cw[tap:tap + 1, :] * u_ref[hr - d:hr - d + tm, :]
        return y

    def up(j, buf):
        buf[0][...] = jnp.dot(hext, wg_ref[j], preferred_element_type=F32)
        buf[1][...] = jnp.dot(hext, wv_ref[j], preferred_element_type=F32)

    def down(j, buf):
        g = conv(buf[0], cwg_ref[j], cbg_ref[j])
        val = conv(buf[1], cwv_ref[j], cbv_ref[j])
        act = (_gelu_tanh(g) * val).astype(BF16)
        o_ref[0] += jnp.dot(act, wd_ref[j], preferred_element_type=F32)

    bufs = ((uga_ref, uva_ref), (ugb_ref, uvb_ref))
    up(0, bufs[0])
    for j in range(nj):
        if j + 1 < nj:
            up(j + 1, bufs[(j + 1) % 2])
        down(j, bufs[j % 2])


def _ffn(h2, x1, wg, wv, cwg, cwv, cbg, cbv, wd):
    b, s, d = x1.shape
    tm, tn = FFN_ROWS, FFN_COLS
    nj = D_FF // tn
    hr = SUBLANES_BF16
    row_blk = pl.BlockSpec((1, tm, d), lambda bi, i: (bi, i, 0))
    halo_blk = pl.BlockSpec((1, hr, d),
                            lambda bi, i: (bi, jnp.maximum(i * (tm // hr) - 1, 0), 0))
    resident = lambda shape: pl.BlockSpec(shape, lambda bi, i: (0,) * len(shape),
                                          pipeline_mode=pl.Buffered(1))
    return pl.pallas_call(
        _ffn_kernel,
        grid=(b, s // tm),
        in_specs=[row_blk, halo_blk, row_blk,
                  resident((nj, d, tn)), resident((nj, d, tn)),
                  resident((nj, FFN_CONV_WIDTH, tn)), resident((nj, FFN_CONV_WIDTH, tn)),
                  resident((nj, 1, tn)), resident((nj, 1, tn)),
                  resident((nj, tn, d))],
        out_specs=row_blk,
        out_shape=jax.ShapeDtypeStruct((b, s, d), F32),
        scratch_shapes=[pltpu.VMEM((hr + tm, tn), F32)] * 4,
        compiler_params=pltpu.CompilerParams(
            dimension_semantics=("arbitrary", "arbitrary"),
            vmem_limit_bytes=VMEM_LIMIT_BYTES),
        name="ffn",
    )(h2, h2, x1, wg, wv, cwg, cwv, cbg, cbv, wd)


def _rope_tables(seq):
    pos = jnp.arange(seq, dtype=F32)
    inv_freq = ROPE_THETA ** (-(jnp.arange(0, ROT_DIM, 2, dtype=F32) / ROT_DIM))
    ang = pos[:, None] * inv_freq[None, :]
    cos, sin = jnp.cos(ang), jnp.sin(ang)
    half = ROT_DIM // 2
    pad = QK_HEAD_DIM - ROT_DIM
    one = jnp.ones((seq, pad), F32)
    zero_h = jnp.zeros((seq, half), F32)
    zero_p = jnp.zeros((seq, pad), F32)
    rc = jnp.concatenate([cos, cos, one], axis=1)
    rs1 = jnp.concatenate([-sin, zero_h, zero_p], axis=1)
    rs2 = jnp.concatenate([zero_h, sin, zero_p], axis=1)
    return tuple(jnp.tile(t, (1, 2)) for t in (rc, rs1, rs2))


def _block_diag(w):
    h, i, j = w.shape
    eye = jnp.eye(h, dtype=w.dtype)
    return (eye[:, None, :, None] * w[:, :, None, :]).reshape(h * i, h * j)


def kernel(x, norm1_w, w_in, q_norm_w, k_norm_w, lambda_q1, lambda_k1, lambda_q2, lambda_k2,
           subln_w, conv_rnn_w, conv_rnn_b, w_gate_a, b_gate_a, w_gate_x, b_gate_x, lru_lambda,
           rnn_norm_w, w_out, norm2_w, w_up, conv_ffn_w, conv_ffn_b, w_down):
    b, s, d = x.shape
    rc, rs1, rs2 = _rope_tables(s)
    nj = D_FF // FFN_COLS
    row = lambda v: v.reshape(1, -1)
    for l in range(DEPTH):
        lambda_init = 0.8 - 0.6 * math.exp(-0.3 * l)
        qt, k, vt, xr, gr = _in_proj(
            x, row(norm1_w[l]), w_in[l].astype(BF16),
            row(jnp.tile(q_norm_w[l], 2)), row(jnp.tile(k_norm_w[l], 2)), rc, rs1, rs2)
        attn = _attention(qt, k, vt, row(lambda_q1[l]), row(lambda_k1[l]), row(lambda_q2[l]),
                          row(lambda_k2[l]), row(subln_w[l]), lambda_init)
        y = _rglru(xr, gr, conv_rnn_w[l], row(conv_rnn_b[l]),
                   _block_diag(w_gate_a[l]).astype(BF16), row(b_gate_a[l]),
                   _block_diag(w_gate_x[l]).astype(BF16), row(b_gate_x[l]),
                   row(lru_lambda[l]), row(rnn_norm_w[l]))
        x1, h2 = _out_proj(attn, y, x, w_out[l].astype(BF16), row(norm2_w[l]))
        chunks = lambda w: w.reshape(d, nj, FFN_COLS).transpose(1, 0, 2)
        cchunks = lambda w: w.reshape(-1, nj, FFN_COLS).transpose(1, 0, 2)
        wu = w_up[l].astype(BF16)
        x = _ffn(h2, x1, chunks(wu[:, :D_FF]), chunks(wu[:, D_FF:]),
                 cchunks(conv_ffn_w[l][:, :D_FF]), cchunks(conv_ffn_w[l][:, D_FF:]),
                 cchunks(conv_ffn_b[l][None, :D_FF]), cchunks(conv_ffn_b[l][None, D_FF:]),
                 w_down[l].astype(BF16).reshape(nj, FFN_COLS, d))
    return x
```

```python
import functools
import math

import jax
import jax.numpy as jnp
from jax import lax
from jax.experimental import pallas as pl
from jax.experimental.pallas import tpu as pltpu

D_MODEL = 1024
DEPTH = 1
ATTN_HEADS = 4
QK_HEAD_DIM = 64
V_HEAD_DIM = 2 * QK_HEAD_DIM
ATTN_WIDTH = ATTN_HEADS * V_HEAD_DIM
QK_WIDTH = ATTN_HEADS * 2 * QK_HEAD_DIM
ROT_DIM = QK_HEAD_DIM // 4
ROPE_THETA = 500000.0
RNN_WIDTH = D_MODEL - ATTN_WIDTH
RNN_HEADS = 8
RNN_BLOCK = RNN_WIDTH // RNN_HEADS
RNN_CONV_WIDTH = 4
LRU_C = 8.0
D_FF = 3 * D_MODEL
FFN_CONV_WIDTH = 3
EPS = 1e-6

LANES = 128
SUBLANES_F32 = 8
SUBLANES_BF16 = 16
VMEM_LIMIT_BYTES = 56 * 1024 * 1024

PROJ_ROWS = 512
ATTN_BLOCK = 512
ATTN_GROUP = 4
RNN_ROWS = 512
FFN_ROWS = 512
FFN_COLS = 512

LOG2E = math.log2(math.e)
F32 = jnp.float32
BF16 = jnp.bfloat16


def _gelu_tanh(x):
    c = math.sqrt(2.0 / math.pi)
    return 0.5 * x * (1.0 + jnp.tanh(c * (x + 0.044715 * (x * x * x))))


def _in_proj_kernel(x_ref, n1_ref, w_ref, qw_ref, kw_ref, rc_ref, rs1_ref, rs2_ref,
                    qt_ref, k_ref, vt_ref, xr_ref, gr_ref):
    x = x_ref[0]
    ms = jnp.mean(x * x, axis=-1, keepdims=True)
    h = (x * lax.rsqrt(ms + EPS) * n1_ref[...]).astype(BF16)
    lane = lax.broadcasted_iota(jnp.int32, (1, LANES), 1)
    lo = lane < QK_HEAD_DIM
    rc, rs1, rs2 = rc_ref[...], rs1_ref[...], rs2_ref[...]

    def qk_section(col0, nw_ref, out_ref, out_scale, transposed):
        p = jnp.dot(h, w_ref[:, col0:col0 + QK_WIDTH], preferred_element_type=F32)
        nw = nw_ref[...]
        for hd in range(ATTN_HEADS):
            y = p[:, LANES * hd:LANES * (hd + 1)]
            sq = y * y
            s_all = jnp.sum(sq, axis=-1, keepdims=True)
            s_lo = jnp.sum(jnp.where(lo, sq, 0.0), axis=-1, keepdims=True)
            msq = jnp.where(lo, s_lo, s_all - s_lo) * (1.0 / QK_HEAD_DIM)
            y = y * lax.rsqrt(msq + EPS) * nw
            up = pltpu.roll(y, LANES - ROT_DIM // 2, axis=1)
            dn = pltpu.roll(y, ROT_DIM // 2, axis=1)
            y = y * rc + up * rs1 + dn * rs2
            if out_scale != 1.0:
                y = y * out_scale
            if transposed:
                out_ref[0, LANES * hd:LANES * (hd + 1), :] = y.T.astype(BF16)
            else:
                out_ref[0, :, LANES * hd:LANES * (hd + 1)] = y.astype(BF16)

    qk_section(0, qw_ref, qt_ref, LOG2E * QK_HEAD_DIM ** -0.5, True)
    qk_section(QK_WIDTH, kw_ref, k_ref, 1.0, False)
    c = 2 * QK_WIDTH
    pv = jnp.dot(h, w_ref[:, c:c + ATTN_WIDTH], preferred_element_type=F32)
    for hd in range(ATTN_HEADS):
        vt_ref[0, LANES * hd:LANES * (hd + 1), :] = (
            pv[:, LANES * hd:LANES * (hd + 1)].T.astype(BF16))
    c += ATTN_WIDTH
    xr_ref[0] = jnp.dot(h, w_ref[:, c:c + RNN_WIDTH], preferred_element_type=F32)
    c += RNN_WIDTH
    gr_ref[0] = jnp.dot(h, w_ref[:, c:c + RNN_WIDTH], preferred_element_type=F32)


def _in_proj(x, n1, w_in, qw2, kw2, rc, rs1, rs2):
    b, s, d = x.shape
    tm = PROJ_ROWS
    n_out = w_in.shape[1]
    row_blk = lambda w: pl.BlockSpec((1, tm, w), lambda bi, i: (bi, i, 0))
    col_blk = lambda w: pl.BlockSpec((1, w, tm), lambda bi, i: (bi, 0, i))
    const = lambda shape: pl.BlockSpec(shape, lambda bi, i: (0,) * len(shape))
    rope_blk = pl.BlockSpec((tm, LANES), lambda bi, i: (i, 0))
    return pl.pallas_call(
        _in_proj_kernel,
        grid=(b, s // tm),
        in_specs=[row_blk(d), const((1, d)), const((d, n_out)), const((1, LANES)),
                  const((1, LANES)), rope_blk, rope_blk, rope_blk],
        out_specs=[col_blk(QK_WIDTH), row_blk(QK_WIDTH), col_blk(ATTN_WIDTH),
                   row_blk(RNN_WIDTH), row_blk(RNN_WIDTH)],
        out_shape=[jax.ShapeDtypeStruct((b, QK_WIDTH, s), BF16),
                   jax.ShapeDtypeStruct((b, s, QK_WIDTH), BF16),
                   jax.ShapeDtypeStruct((b, ATTN_WIDTH, s), BF16),
                   jax.ShapeDtypeStruct((b, s, RNN_WIDTH), F32),
                   jax.ShapeDtypeStruct((b, s, RNN_WIDTH), F32)],
        compiler_params=pltpu.CompilerParams(
            dimension_semantics=("arbitrary", "arbitrary"),
            vmem_limit_bytes=VMEM_LIMIT_BYTES),
        name="in_proj",
    )(x, n1, w_in, qw2, kw2, rc, rs1, rs2)


def _attn_kernel(qlo_ref, qhi_ref, k_ref, vt_ref, lq1_ref, lk1_ref, lq2_ref, lk2_ref, sw_ref,
                 o_ref, qz_ref, m_ref, l_ref, acc_ref, sa_ref, mxa_ref, sb_ref,
                 mxb_ref, *, lambda_init, n_blocks):
    t = ATTN_BLOCK
    lo = pl.program_id(2)
    hi = n_blocks - 1 - lo
    dim = lax.broadcasted_iota(jnp.int32, (LANES, 1), 0)
    for w, q_ref in enumerate((qlo_ref, qhi_ref)):
        qt = q_ref[0]
        zero = jnp.zeros_like(qt)
        qz_ref[w, 0] = jnp.where(dim < QK_HEAD_DIM, qt, zero)
        qz_ref[w, 1] = jnp.where(dim >= QK_HEAD_DIM, qt, zero)

    m_ref[...] = jnp.full(m_ref.shape, -1e30, F32)
    l_ref[...] = jnp.zeros(l_ref.shape, F32)
    acc_ref[...] = jnp.zeros(acc_ref.shape, F32)

    def scores(buf, w, j, masked):
        s_ref, mx_ref = buf
        r0 = pl.multiple_of(j * t, t)
        kb = k_ref[0, pl.ds(r0, t), :]
        for c in range(2):
            s = jnp.dot(kb, qz_ref[w, c], preferred_element_type=F32)
            if masked:
                key = lax.broadcasted_iota(jnp.int32, (t, t), 0)
                qry = lax.broadcasted_iota(jnp.int32, (t, t), 1)
                s = jnp.where(key <= qry, s, -jnp.inf)
            s_ref[c] = s
            mx_ref[c] = jnp.max(s, axis=0, keepdims=True)

    def accumulate(buf, w, j):
        s_ref, mx_ref = buf
        r0 = pl.multiple_of(j * t, t)
        vtb = vt_ref[0, :, pl.ds(r0, t)]
        for c in range(2):
            m_prev = m_ref[w, c]
            m_new = jnp.maximum(m_prev, mx_ref[c])
            alpha = jnp.exp2(m_prev - m_new)
            e = jnp.exp2(s_ref[c] - m_new)
            l_ref[w, c] = alpha * l_ref[w, c] + jnp.sum(e, axis=0, keepdims=True)
            acc_ref[w, c] = alpha * acc_ref[w, c] + jnp.dot(vtb, e.astype(BF16),
                                                            preferred_element_type=F32)
            m_ref[w, c] = m_new

    n_items = n_blocks + 1
    group = ATTN_GROUP
    assert (n_items - 1) % group == 0

    def item(n):
        if isinstance(n, int) and n < 2:
            return (n, (lo, hi)[n])
        is_hi = (n - 2) >= lo
        return (is_hi.astype(jnp.int32), jnp.where(is_hi, n - 2 - lo, n - 2))

    bufs = ((sa_ref, mxa_ref), (sb_ref, mxb_ref))

    def run_group(first, masked_until):
        for r in range(group):
            n = first + r
            scores(bufs[(r + 1) % 2], *item(n + 1),
                   isinstance(n, int) and n + 1 < masked_until)
            accumulate(bufs[r % 2], *item(n))

    scores(bufs[0], *item(0), True)
    run_group(0, 2)

    def trip(i, carry):
        run_group(i * group, 0)
        return carry

    lax.fori_loop(1, (n_items - 1) // group, trip, 0)
    accumulate(bufs[0], *item(n_items - 1))

    lam = (jnp.exp(jnp.sum(lq1_ref[...] * lk1_ref[...], axis=-1, keepdims=True))
           - jnp.exp(jnp.sum(lq2_ref[...] * lk2_ref[...], axis=-1, keepdims=True))
           + lambda_init)
    for w in range(2):
        o = acc_ref[w, 0] * (1.0 / l_ref[w, 0]) - acc_ref[w, 1] * (lam / l_ref[w, 1])
        ms = jnp.mean(o * o, axis=0, keepdims=True)
        o = o * (lax.rsqrt(ms + EPS) * (1.0 - lambda_init))
        o_ref[0, w] = (o.T * sw_ref[...]).astype(BF16)


def _attention(qt, k, vt, lq1, lk1, lq2, lk2, sw, lambda_init):
    b, s, _ = k.shape
    t = ATTN_BLOCK
    nb = s // t
    half = nb // 2
    vec = lambda n: pl.BlockSpec((1, n), lambda bi, h, p: (0, 0))
    return pl.pallas_call(
        functools.partial(_attn_kernel, lambda_init=lambda_init, n_blocks=nb),
        grid=(b, ATTN_HEADS, half),
        in_specs=[pl.BlockSpec((1, LANES, t), lambda bi, h, p: (bi, h, p)),
                  pl.BlockSpec((1, LANES, t), lambda bi, h, p: (bi, h, nb - 1 - p)),
                  pl.BlockSpec((1, s, LANES), lambda bi, h, p: (bi, 0, h)),
                  pl.BlockSpec((1, LANES, s), lambda bi, h, p: (bi, h, 0)),
                  vec(QK_HEAD_DIM), vec(QK_HEAD_DIM), vec(QK_HEAD_DIM), vec(QK_HEAD_DIM),
                  vec(V_HEAD_DIM)],
        out_specs=pl.BlockSpec((1, 2, t, LANES), lambda bi, h, p: (bi, 0, p, h)),
        out_shape=jax.ShapeDtypeStruct((b, 2, s // 2, ATTN_WIDTH), BF16),
        scratch_shapes=[pltpu.VMEM((2, 2, LANES, t), BF16),
                        pltpu.VMEM((2, 2, 1, t), F32), pltpu.VMEM((2, 2, 1, t), F32),
                        pltpu.VMEM((2, 2, V_HEAD_DIM, t), F32),
                        pltpu.VMEM((2, t, t), F32), pltpu.VMEM((2, 1, t), F32),
                        pltpu.VMEM((2, t, t), F32), pltpu.VMEM((2, 1, t), F32)],
        compiler_params=pltpu.CompilerParams(
            dimension_semantics=("arbitrary", "arbitrary", "arbitrary"),
            vmem_limit_bytes=VMEM_LIMIT_BYTES),
        name="diff_attn",
    )(qt, qt, k, vt, lq1, lk1, lq2, lk2, sw)


def _rglru_kernel(xr_ref, gr_ref, cw_ref, cb_ref, wa_ref, ba_ref, wx_ref, bx_ref, lam_ref,
                  nw_ref, y_ref, ext_ref, a_ref, b_ref, hc_ref):
    ts = RNN_ROWS
    halo = SUBLANES_F32

    @pl.when(pl.program_id(1) == 0)
    def _():
        ext_ref[0:halo, :] = jnp.zeros((halo, RNN_WIDTH), F32)
        hc_ref[...] = jnp.zeros((halo, RNN_WIDTH), F32)

    x = xr_ref[0]
    ext_ref[halo:halo + ts, :] = x
    xc = cb_ref[...] + cw_ref[RNN_CONV_WIDTH - 1:RNN_CONV_WIDTH, :] * x
    for d in range(1, RNN_CONV_WIDTH):
        tap = RNN_CONV_WIDTH - 1 - d
        xc = xc + cw_ref[tap:tap + 1, :] * ext_ref[halo - d:halo - d + ts, :]
    ext_ref[0:halo, :] = x[ts - halo:ts, :]

    xb = xc.astype(BF16)
    r = jax.nn.sigmoid(jnp.dot(xb, wa_ref[...], preferred_element_type=F32) + ba_ref[...])
    ig = jax.nn.sigmoid(jnp.dot(xb, wx_ref[...], preferred_element_type=F32) + bx_ref[...])
    z = -lam_ref[...]
    softplus = jnp.maximum(z, 0.0) + jnp.log1p(jnp.exp(-jnp.abs(z)))
    log_a = -LRU_C * r * softplus
    a = jnp.exp(log_a)
    w = 1.0 - a * a
    mult = jnp.where(w > 0.0, w * lax.rsqrt(w), 0.0)
    u = mult * ig * xc

    tiles = (ts // SUBLANES_F32, SUBLANES_F32, RNN_WIDTH)
    a = a.reshape(tiles)
    u = u.reshape(tiles)
    sub = lax.broadcasted_iota(jnp.int32, (1, SUBLANES_F32, 1), 1)
    d = 1
    while d < SUBLANES_F32:
        keep = sub >= d
        a_prev = jnp.where(keep, pltpu.roll(a, d, axis=1), 1.0)
        u_prev = jnp.where(keep, pltpu.roll(u, d, axis=1), 0.0)
        u = a * u_prev + u
        a = a * a_prev
        d *= 2
    a_ref[...] = a.reshape(ts, RNN_WIDTH)
    b_ref[...] = u.reshape(ts, RNN_WIDTH)

    def tile(j, h_prev):
        r0 = pl.multiple_of(j * SUBLANES_F32, SUBLANES_F32)
        h = b_ref[pl.ds(r0, SUBLANES_F32), :] + a_ref[pl.ds(r0, SUBLANES_F32), :] * h_prev
        b_ref[pl.ds(r0, SUBLANES_F32), :] = h
        return jnp.broadcast_to(h[SUBLANES_F32 - 1:SUBLANES_F32, :], h.shape)

    hc_ref[...] = lax.fori_loop(0, ts // SUBLANES_F32, tile, hc_ref[...], unroll=8)

    y = b_ref[...] * _gelu_tanh(gr_ref[0])
    ms = jnp.mean(y * y, axis=-1, keepdims=True)
    y_ref[0] = (y * lax.rsqrt(ms + EPS) * nw_ref[...]).astype(BF16)


def _rglru(xr, gr, cw, cb, wa, ba, wx, bx, lam, nw):
    b, s, c = xr.shape
    ts = RNN_ROWS
    row_blk = pl.BlockSpec((1, ts, c), lambda bi, i: (bi, i, 0))
    const = lambda shape: pl.BlockSpec(shape, lambda bi, i: (0,) * len(shape))
    return pl.pallas_call(
        _rglru_kernel,
        grid=(b, s // ts),
        in_specs=[row_blk, row_blk, const((RNN_CONV_WIDTH, c)), const((1, c)),
                  const((c, c)), const((1, c)), const((c, c)), const((1, c)),
                  const((1, c)), const((1, c))],
        out_specs=row_blk,
        out_shape=jax.ShapeDtypeStruct((b, s, c), BF16),
        scratch_shapes=[pltpu.VMEM((ts + SUBLANES_F32, c), F32),
                        pltpu.VMEM((ts, c), F32), pltpu.VMEM((ts, c), F32),
                        pltpu.VMEM((SUBLANES_F32, c), F32)],
        compiler_params=pltpu.CompilerParams(
            dimension_semantics=("arbitrary", "arbitrary"),
            vmem_limit_bytes=VMEM_LIMIT_BYTES),
        name="rglru",
    )(xr, gr, cw, cb, wa, ba, wx, bx, lam, nw)


def _mlp_kernel(attn_ref, attn_h_ref, y_ref, y_h_ref, x_ref, x_h_ref, wo_ref, n2_ref,
                wu_ref, cw_ref, cb_ref, wd_ref, o_ref, uga_ref, uva_ref, ugb_ref, uvb_ref):
    tm = FFN_ROWS
    tn = FFN_COLS
    hr = SUBLANES_BF16
    nj = D_FF // tn

    attn = jnp.concatenate([attn_h_ref[0, 0], attn_ref[0, 0]], axis=0)
    yrnn = jnp.concatenate([y_h_ref[0], y_ref[0]], axis=0)
    xin = jnp.concatenate([x_h_ref[0], x_ref[0]], axis=0)
    x1 = xin + jnp.dot(attn, wo_ref[0:ATTN_WIDTH, :], preferred_element_type=F32)
    x1 = x1 + jnp.dot(yrnn, wo_ref[ATTN_WIDTH:, :], preferred_element_type=F32)
    ms = jnp.mean(x1 * x1, axis=-1, keepdims=True)
    h2 = x1 * lax.rsqrt(ms + EPS) * n2_ref[...]
    row = lax.broadcasted_iota(jnp.int32, (hr + tm, 1), 0)
    live = jnp.logical_or(row >= hr, pl.program_id(1) > 0)
    hext = jnp.where(live, h2, 0.0).astype(BF16)
    o_ref[0] = x1[hr:, :]

    def conv(u_ref, c0):
        y = cb_ref[:, c0:c0 + tn] + (
            cw_ref[FFN_CONV_WIDTH - 1:FFN_CONV_WIDTH, c0:c0 + tn] * u_ref[hr:hr + tm, :])
        for d in range(1, FFN_CONV_WIDTH):
            tap = FFN_CONV_WIDTH - 1 - d
            y = y + cw_ref[tap:tap + 1, c0:c0 + tn] * u_ref[hr - d:hr - d + tm, :]
        return y

    def up(j, buf):
        g0, v0 = j * tn, D_FF + j * tn
        buf[0][...] = jnp.dot(hext, wu_ref[:, g0:g0 + tn], preferred_element_type=F32)
        buf[1][...] = jnp.dot(hext, wu_ref[:, v0:v0 + tn], preferred_element_type=F32)

    def down(j, buf):
        g = conv(buf[0], j * tn)
        val = conv(buf[1], D_FF + j * tn)
        act = (_gelu_tanh(g) * val).astype(BF16)
        o_ref[0] += jnp.dot(act, wd_ref[j * tn:(j + 1) * tn, :], preferred_element_type=F32)

    bufs = ((uga_ref, uva_ref), (ugb_ref, uvb_ref))
    up(0, bufs[0])
    for j in range(nj):
        if j + 1 < nj:
            up(j + 1, bufs[(j + 1) % 2])
        down(j, bufs[j % 2])


def _mlp(attn, y, x, w_out, n2, w_up, cw, cb, w_down):
    b, s, d = x.shape
    tm, tn = FFN_ROWS, FFN_COLS
    hr = SUBLANES_BF16
    row_blk = lambda w: pl.BlockSpec((1, tm, w), lambda bi, i: (bi, i, 0))
    hist_blk = lambda w: pl.BlockSpec(
        (1, hr, w), lambda bi, i: (bi, jnp.maximum(i * (tm // hr) - 1, 0), 0))
    assert tm == ATTN_BLOCK
    nblk = s // tm
    nh = nblk // 2
    fold = lambda i: jnp.where(i < nh, i, nblk - 1 - i)
    attn_blk = pl.BlockSpec((1, 1, tm, ATTN_WIDTH), lambda bi, i: (bi, i // nh, fold(i), 0))

    def attn_hist_map(bi, i):
        ip = jnp.maximum(i - 1, 0)
        return (bi, ip // nh, fold(ip) * (tm // hr) + (tm // hr - 1), 0)

    attn_hist_blk = pl.BlockSpec((1, 1, hr, ATTN_WIDTH), attn_hist_map)
    resident = lambda shape: pl.BlockSpec(shape, lambda bi, i: (0,) * len(shape),
                                          pipeline_mode=pl.Buffered(1))
    return pl.pallas_call(
        _mlp_kernel,
        grid=(b, s // tm),
        in_specs=[attn_blk, attn_hist_blk, row_blk(RNN_WIDTH),
                  hist_blk(RNN_WIDTH), row_blk(d), hist_blk(d),
                  resident((d, d)), resident((1, d)), resident((d, 2 * D_FF)),
                  resident((FFN_CONV_WIDTH, 2 * D_FF)), resident((1, 2 * D_FF)),
                  resident((D_FF, d))],
        out_specs=row_blk(d),
        out_shape=jax.ShapeDtypeStruct((b, s, d), F32),
        scratch_shapes=[pltpu.VMEM((hr + tm, tn), F32)] * 4,
        compiler_params=pltpu.CompilerParams(
            dimension_semantics=("arbitrary", "arbitrary"),
            vmem_limit_bytes=VMEM_LIMIT_BYTES),
        name="mlp",
    )(attn, attn, y, y, x, x, w_out, n2, w_up, cw, cb, w_down)


def _rope_tables(seq):
    pos = jnp.arange(seq, dtype=F32)
    inv_freq = ROPE_THETA ** (-(jnp.arange(0, ROT_DIM, 2, dtype=F32) / ROT_DIM))
    ang = pos[:, None] * inv_freq[None, :]
    cos, sin = jnp.cos(ang), jnp.sin(ang)
    half = ROT_DIM // 2
    pad = QK_HEAD_DIM - ROT_DIM
    one = jnp.ones((seq, pad), F32)
    zero_h = jnp.zeros((seq, half), F32)
    zero_p = jnp.zeros((seq, pad), F32)
    rc = jnp.concatenate([cos, cos, one], axis=1)
    rs1 = jnp.concatenate([-sin, zero_h, zero_p], axis=1)
    rs2 = jnp.concatenate([zero_h, sin, zero_p], axis=1)
    return tuple(jnp.tile(t, (1, 2)) for t in (rc, rs1, rs2))


def _block_diag(w):
    h, i, j = w.shape
    eye = jnp.eye(h, dtype=w.dtype)
    return (eye[:, None, :, None] * w[:, :, None, :]).reshape(h * i, h * j)


def kernel(x, norm1_w, w_in, q_norm_w, k_norm_w, lambda_q1, lambda_k1, lambda_q2, lambda_k2,
           subln_w, conv_rnn_w, conv_rnn_b, w_gate_a, b_gate_a, w_gate_x, b_gate_x, lru_lambda,
           rnn_norm_w, w_out, norm2_w, w_up, conv_ffn_w, conv_ffn_b, w_down):
    b, s, d = x.shape
    rc, rs1, rs2 = _rope_tables(s)
    nj = D_FF // FFN_COLS
    row = lambda v: v.reshape(1, -1)
    for l in range(DEPTH):
        lambda_init = 0.8 - 0.6 * math.exp(-0.3 * l)
        qt, k, vt, xr, gr = _in_proj(
            x, row(norm1_w[l]), w_in[l].astype(BF16),
            row(jnp.tile(q_norm_w[l], 2)), row(jnp.tile(k_norm_w[l], 2)), rc, rs1, rs2)
        attn = _attention(qt, k, vt, row(lambda_q1[l]), row(lambda_k1[l]), row(lambda_q2[l]),
                          row(lambda_k2[l]), row(subln_w[l]), lambda_init)
        y = _rglru(xr, gr, conv_rnn_w[l], row(conv_rnn_b[l]),
                   _block_diag(w_gate_a[l]).astype(BF16), row(b_gate_a[l]),
                   _block_diag(w_gate_x[l]).astype(BF16), row(b_gate_x[l]),
                   row(lru_lambda[l]), row(rnn_norm_w[l]))
        x = _mlp(attn, y, x, w_out[l].astype(BF16), row(norm2_w[l]), w_up[l].astype(BF16),
                 conv_ffn_w[l], row(conv_ffn_b[l]), w_down[l].astype(BF16))
    return x
```

```python
import functools
import math

import jax
import jax.numpy as jnp
from jax import lax
from jax.experimental import pallas as pl
from jax.experimental.pallas import tpu as pltpu

D_MODEL = 1024
DEPTH = 1
ATTN_HEADS = 4
QK_HEAD_DIM = 64
V_HEAD_DIM = 2 * QK_HEAD_DIM
ATTN_WIDTH = ATTN_HEADS * V_HEAD_DIM
QK_WIDTH = ATTN_HEADS * 2 * QK_HEAD_DIM
ROT_DIM = QK_HEAD_DIM // 4
ROPE_THETA = 500000.0
RNN_WIDTH = D_MODEL - ATTN_WIDTH
RNN_HEADS = 8
RNN_BLOCK = RNN_WIDTH // RNN_HEADS
RNN_CONV_WIDTH = 4
LRU_C = 8.0
D_FF = 3 * D_MODEL
FFN_CONV_WIDTH = 3
EPS = 1e-6

LANES = 128
SUBLANES_F32 = 8
SUBLANES_BF16 = 16
VMEM_LIMIT_BYTES = 56 * 1024 * 1024

PROJ_ROWS = 512
ATTN_BLOCK = 512
ATTN_GROUP = 4
RNN_ROWS = 512
FFN_ROWS = 512
FFN_COLS = 512

LOG2E = math.log2(math.e)
F32 = jnp.float32
BF16 = jnp.bfloat16


def _gelu_gate(x, half_v):
    c = math.sqrt(2.0 / math.pi)
    t = jnp.tanh(x * (c + (c * 0.044715) * (x * x)))
    h = x * half_v
    return h + h * t


def _in_proj_kernel(x_ref, n1_ref, wt_ref, w_ref, qw_ref, kw_ref, ct_ref, st_ref, rc_ref,
                    rs1_ref, rs2_ref, qt_ref, k_ref, vt_ref, xr_ref, gr_ref):
    x = x_ref[0]
    ms = jnp.mean(x * x, axis=-1, keepdims=True)
    h = (x * lax.rsqrt(ms + EPS) * n1_ref[...]).astype(BF16)

    half = ROT_DIM // 2
    lane = lax.broadcasted_iota(jnp.int32, (1, LANES), 1)
    lo = lane < QK_HEAD_DIM
    rc, rs1, rs2 = rc_ref[...], rs1_ref[...], rs2_ref[...]
    pk = jnp.dot(h, w_ref[:, 0:QK_WIDTH], preferred_element_type=F32)
    kw = kw_ref[...]
    for hd in range(ATTN_HEADS):
        y = pk[:, LANES * hd:LANES * (hd + 1)]
        sq = y * y
        s_all = jnp.sum(sq, axis=-1, keepdims=True)
        s_lo = jnp.sum(jnp.where(lo, sq, 0.0), axis=-1, keepdims=True)
        msq = jnp.where(lo, s_lo, s_all - s_lo) * (1.0 / QK_HEAD_DIM)
        y = y * lax.rsqrt(msq + EPS) * kw
        up = pltpu.roll(y, LANES - half, axis=1)
        dn = pltpu.roll(y, half, axis=1)
        k_ref[0, :, LANES * hd:LANES * (hd + 1)] = (y * rc + up * rs1 + dn * rs2).astype(BF16)
    c = QK_WIDTH
    xr_ref[0] = jnp.dot(h, w_ref[:, c:c + RNN_WIDTH], preferred_element_type=F32)
    c += RNN_WIDTH
    gr_ref[0] = jnp.dot(h, w_ref[:, c:c + RNN_WIDTH], preferred_element_type=F32)

    nt = (((1,), (1,)), ((), ()))
    vt_ref[0] = lax.dot_general(wt_ref[QK_WIDTH:, :], h, nt,
                                preferred_element_type=F32).astype(BF16)
    pq = lax.dot_general(wt_ref[0:QK_WIDTH, :], h, nt, preferred_element_type=F32)
    ct, st = ct_ref[...], st_ref[...]
    q_scale = LOG2E * QK_HEAD_DIM ** -0.5
    for blk in range(QK_WIDTH // QK_HEAD_DIM):
        r0 = blk * QK_HEAD_DIM
        z = pq[r0:r0 + QK_HEAD_DIM, :]
        msq = jnp.mean(z * z, axis=0, keepdims=True)
        z = z * lax.rsqrt(msq + EPS) * qw_ref[...]
        x1, x2 = z[0:half, :], z[half:ROT_DIM, :]
        z = jnp.concatenate([x1 * ct - x2 * st, x2 * ct + x1 * st, z[ROT_DIM:, :]], axis=0)
        qt_ref[0, r0:r0 + QK_HEAD_DIM, :] = (z * q_scale).astype(BF16)


def _in_proj(x, n1, wt, w_rest, qw_col, kw2, ct, st, rc, rs1, rs2):
    b, s, d = x.shape
    tm = PROJ_ROWS
    row_blk = lambda w: pl.BlockSpec((1, tm, w), lambda bi, i: (bi, i, 0))
    col_blk = lambda w: pl.BlockSpec((1, w, tm), lambda bi, i: (bi, 0, i))
    const = lambda shape: pl.BlockSpec(shape, lambda bi, i: (0,) * len(shape))
    rope_blk = pl.BlockSpec((tm, LANES), lambda bi, i: (i, 0))
    rope_t_blk = pl.BlockSpec((ROT_DIM // 2, tm), lambda bi, i: (0, i))
    return pl.pallas_call(
        _in_proj_kernel,
        grid=(b, s // tm),
        in_specs=[row_blk(d), const((1, d)), const(wt.shape), const(w_rest.shape),
                  const((QK_HEAD_DIM, 1)), const((1, LANES)), rope_t_blk, rope_t_blk,
                  rope_blk, rope_blk, rope_blk],
        out_specs=[col_blk(QK_WIDTH), row_blk(QK_WIDTH), col_blk(ATTN_WIDTH),
                   row_blk(RNN_WIDTH), row_blk(RNN_WIDTH)],
        out_shape=[jax.ShapeDtypeStruct((b, QK_WIDTH, s), BF16),
                   jax.ShapeDtypeStruct((b, s, QK_WIDTH), BF16),
                   jax.ShapeDtypeStruct((b, ATTN_WIDTH, s), BF16),
                   jax.ShapeDtypeStruct((b, s, RNN_WIDTH), F32),
                   jax.ShapeDtypeStruct((b, s, RNN_WIDTH), F32)],
        compiler_params=pltpu.CompilerParams(
            dimension_semantics=("arbitrary", "arbitrary"),
            vmem_limit_bytes=VMEM_LIMIT_BYTES),
        name="in_proj",
    )(x, n1, wt, w_rest, qw_col, kw2, ct, st, rc, rs1, rs2)


def _attn_kernel(qlo_ref, qhi_ref, k_ref, vt_ref, lq1_ref, lk1_ref, lq2_ref, lk2_ref, sw_ref,
                 o_ref, qz_ref, m_ref, l_ref, acc_ref, sa_ref, mxa_ref, sb_ref,
                 mxb_ref, *, lambda_init, n_blocks):
    t = ATTN_BLOCK
    lo = pl.program_id(2)
    hi = n_blocks - 1 - lo
    dim = lax.broadcasted_iota(jnp.int32, (LANES, 1), 0)
    for w, q_ref in enumerate((qlo_ref, qhi_ref)):
        qt = q_ref[0]
        zero = jnp.zeros_like(qt)
        qz_ref[w, 0] = jnp.where(dim < QK_HEAD_DIM, qt, zero)
        qz_ref[w, 1] = jnp.where(dim >= QK_HEAD_DIM, qt, zero)

    m_ref[...] = jnp.full(m_ref.shape, -1e30, F32)
    l_ref[...] = jnp.zeros(l_ref.shape, F32)
    acc_ref[...] = jnp.zeros(acc_ref.shape, F32)

    def scores(buf, w, j, masked):
        s_ref, mx_ref = buf
        r0 = pl.multiple_of(j * t, t)
        kb = k_ref[0, pl.ds(r0, t), :]
        for c in range(2):
            s = jnp.dot(kb, qz_ref[w, c], preferred_element_type=F32)
            if masked:
                key = lax.broadcasted_iota(jnp.int32, (t, t), 0)
                qry = lax.broadcasted_iota(jnp.int32, (t, t), 1)
                s = jnp.where(key <= qry, s, -jnp.inf)
            s_ref[c] = s
            mx_ref[c] = jnp.max(s, axis=0, keepdims=True)

    def accumulate(buf, w, j):
        s_ref, mx_ref = buf
        r0 = pl.multiple_of(j * t, t)
        vtb = vt_ref[0, :, pl.ds(r0, t)]
        for c in range(2):
            m_prev = m_ref[w, c]
            m_new = jnp.maximum(m_prev, mx_ref[c])
            alpha = jnp.exp2(m_prev - m_new)
            e = jnp.exp2(s_ref[c] - m_new)
            l_ref[w, c] = alpha * l_ref[w, c] + jnp.sum(e, axis=0, keepdims=True)
            acc_ref[w, c] = alpha * acc_ref[w, c] + jnp.dot(vtb, e.astype(BF16),
                                                            preferred_element_type=F32)
            m_ref[w, c] = m_new

    n_items = n_blocks + 1
    group = ATTN_GROUP
    assert (n_items - 1) % group == 0

    def item(n):
        if isinstance(n, int) and n < 2:
            return (n, (lo, hi)[n])
        is_hi = (n - 2) >= lo
        return (is_hi.astype(jnp.int32), jnp.where(is_hi, n - 2 - lo, n - 2))

    bufs = ((sa_ref, mxa_ref), (sb_ref, mxb_ref))

    def run_group(first, masked_until):
        for r in range(group):
            n = first + r
            scores(bufs[(r + 1) % 2], *item(n + 1),
                   isinstance(n, int) and n + 1 < masked_until)
            accumulate(bufs[r % 2], *item(n))

    scores(bufs[0], *item(0), True)
    run_group(0, 2)

    def trip(i, carry):
        run_group(i * group, 0)
        return carry

    lax.fori_loop(1, (n_items - 1) // group, trip, 0)
    accumulate(bufs[0], *item(n_items - 1))

    lam = (jnp.exp(jnp.sum(lq1_ref[...] * lk1_ref[...], axis=-1, keepdims=True))
           - jnp.exp(jnp.sum(lq2_ref[...] * lk2_ref[...], axis=-1, keepdims=True))
           + lambda_init)
    for w in range(2):
        o = acc_ref[w, 0] * (1.0 / l_ref[w, 0]) - acc_ref[w, 1] * (lam / l_ref[w, 1])
        ms = jnp.mean(o * o, axis=0, keepdims=True)
        o = o * (lax.rsqrt(ms + EPS) * (1.0 - lambda_init))
        o_ref[0, w] = (o.T * sw_ref[...]).astype(BF16)


def _attention(qt, k, vt, lq1, lk1, lq2, lk2, sw, lambda_init):
    b, s, _ = k.shape
    t = ATTN_BLOCK
    nb = s // t
    half = nb // 2
    vec = lambda n: pl.BlockSpec((1, n), lambda bi, h, p: (0, 0))
    return pl.pallas_call(
        functools.partial(_attn_kernel, lambda_init=lambda_init, n_blocks=nb),
        grid=(b, ATTN_HEADS, half),
        in_specs=[pl.BlockSpec((1, LANES, t), lambda bi, h, p: (bi, h, p)),
                  pl.BlockSpec((1, LANES, t), lambda bi, h, p: (bi, h, nb - 1 - p)),
                  pl.BlockSpec((1, s, LANES), lambda bi, h, p: (bi, 0, h)),
                  pl.BlockSpec((1, LANES, s), lambda bi, h, p: (bi, h, 0)),
                  vec(QK_HEAD_DIM), vec(QK_HEAD_DIM), vec(QK_HEAD_DIM), vec(QK_HEAD_DIM),
                  vec(V_HEAD_DIM)],
        out_specs=pl.BlockSpec((1, 2, t, LANES), lambda bi, h, p: (bi, 0, p, h)),
        out_shape=jax.ShapeDtypeStruct((b, 2, s // 2, ATTN_WIDTH), BF16),
        scratch_shapes=[pltpu.VMEM((2, 2, LANES, t), BF16),
                        pltpu.VMEM((2, 2, 1, t), F32), pltpu.VMEM((2, 2, 1, t), F32),
                        pltpu.VMEM((2, 2, V_HEAD_DIM, t), F32),
                        pltpu.VMEM((2, t, t), F32), pltpu.VMEM((2, 1, t), F32),
                        pltpu.VMEM((2, t, t), F32), pltpu.VMEM((2, 1, t), F32)],
        compiler_params=pltpu.CompilerParams(
            dimension_semantics=("arbitrary", "arbitrary", "arbitrary"),
            vmem_limit_bytes=VMEM_LIMIT_BYTES),
        name="diff_attn",
    )(qt, qt, k, vt, lq1, lk1, lq2, lk2, sw)


def _rglru_kernel(xr_ref, gr_ref, cw_ref, cb_ref, wa_ref, ba_ref, wx_ref, bx_ref, lam_ref,
                  nw_ref, y_ref, ext_ref, a_ref, b_ref, hc_ref):
    ts = RNN_ROWS
    halo = SUBLANES_F32

    @pl.when(pl.program_id(1) == 0)
    def _():
        ext_ref[0:halo, :] = jnp.zeros((halo, RNN_WIDTH), F32)
        hc_ref[...] = jnp.zeros((halo, RNN_WIDTH), F32)

    x = xr_ref[0]
    ext_ref[halo:halo + ts, :] = x
    xc = cb_ref[...] + cw_ref[RNN_CONV_WIDTH - 1:RNN_CONV_WIDTH, :] * x
    for d in range(1, RNN_CONV_WIDTH):
        tap = RNN_CONV_WIDTH - 1 - d
        xc = xc + cw_ref[tap:tap + 1, :] * ext_ref[halo - d:halo - d + ts, :]
    ext_ref[0:halo, :] = x[ts - halo:ts, :]

    xb = xc.astype(BF16)
    r = jax.nn.sigmoid(jnp.dot(xb, wa_ref[...], preferred_element_type=F32) + ba_ref[...])
    ig = jax.nn.sigmoid(jnp.dot(xb, wx_ref[...], preferred_element_type=F32) + bx_ref[...])
    z = -lam_ref[...]
    softplus = jnp.maximum(z, 0.0) + jnp.log1p(jnp.exp(-jnp.abs(z)))
    log_a = r * (-LRU_C * softplus)
    a = jnp.exp(log_a)
    w = 1.0 - a * a
    mult = jnp.where(w > 0.0, w * lax.rsqrt(w), 0.0)
    u = mult * ig * xc

    tiles = (ts // SUBLANES_F32, SUBLANES_F32, RNN_WIDTH)
    a = a.reshape(tiles)
    u = u.reshape(tiles)
    sub = lax.broadcasted_iota(jnp.int32, (1, SUBLANES_F32, 1), 1)
    d = 1
    while d < SUBLANES_F32:
        keep = sub >= d
        a_prev = jnp.where(keep, pltpu.roll(a, d, axis=1), 1.0)
        u_prev = jnp.where(keep, pltpu.roll(u, d, axis=1), 0.0)
        u = a * u_prev + u
        a = a * a_prev
        d *= 2
    a_ref[...] = a.reshape(ts, RNN_WIDTH)
    b_ref[...] = u.reshape(ts, RNN_WIDTH)

    def tile(j, h_prev):
        r0 = pl.multiple_of(j * SUBLANES_F32, SUBLANES_F32)
        h = b_ref[pl.ds(r0, SUBLANES_F32), :] + a_ref[pl.ds(r0, SUBLANES_F32), :] * h_prev
        b_ref[pl.ds(r0, SUBLANES_F32), :] = h
        return jnp.broadcast_to(h[SUBLANES_F32 - 1:SUBLANES_F32, :], h.shape)

    hc_ref[...] = lax.fori_loop(0, ts // SUBLANES_F32, tile, hc_ref[...], unroll=8)

    y = _gelu_gate(gr_ref[0], 0.5 * b_ref[...])
    ms = jnp.mean(y * y, axis=-1, keepdims=True)
    y_ref[0] = (y * lax.rsqrt(ms + EPS) * nw_ref[...]).astype(BF16)


def _rglru(xr, gr, cw, cb, wa, ba, wx, bx, lam, nw):
    b, s, c = xr.shape
    ts = RNN_ROWS
    row_blk = pl.BlockSpec((1, ts, c), lambda bi, i: (bi, i, 0))
    const = lambda shape: pl.BlockSpec(shape, lambda bi, i: (0,) * len(shape))
    return pl.pallas_call(
        _rglru_kernel,
        grid=(b, s // ts),
        in_specs=[row_blk, row_blk, const((RNN_CONV_WIDTH, c)), const((1, c)),
                  const((c, c)), const((1, c)), const((c, c)), const((1, c)),
                  const((1, c)), const((1, c))],
        out_specs=row_blk,
        out_shape=jax.ShapeDtypeStruct((b, s, c), BF16),
        scratch_shapes=[pltpu.VMEM((ts + SUBLANES_F32, c), F32),
                        pltpu.VMEM((ts, c), F32), pltpu.VMEM((ts, c), F32),
                        pltpu.VMEM((SUBLANES_F32, c), F32)],
        compiler_params=pltpu.CompilerParams(
            dimension_semantics=("arbitrary", "arbitrary"),
            vmem_limit_bytes=VMEM_LIMIT_BYTES),
        name="rglru",
    )(xr, gr, cw, cb, wa, ba, wx, bx, lam, nw)


def _mlp_kernel(attn_ref, attn_h_ref, y_ref, y_h_ref, x_ref, x_h_ref, wo_ref, n2_ref,
                wu_ref, cw_ref, cb_ref, wd_ref, o_ref, uga_ref, uva_ref, ugb_ref, uvb_ref):
    tm = FFN_ROWS
    tn = FFN_COLS
    hr = SUBLANES_BF16
    nj = D_FF // tn

    attn = jnp.concatenate([attn_h_ref[0, 0], attn_ref[0, 0]], axis=0)
    yrnn = jnp.concatenate([y_h_ref[0], y_ref[0]], axis=0)
    xin = jnp.concatenate([x_h_ref[0], x_ref[0]], axis=0)
    x1 = xin + jnp.dot(attn, wo_ref[0:ATTN_WIDTH, :], preferred_element_type=F32)
    x1 = x1 + jnp.dot(yrnn, wo_ref[ATTN_WIDTH:, :], preferred_element_type=F32)
    ms = jnp.mean(x1 * x1, axis=-1, keepdims=True)
    h2 = x1 * lax.rsqrt(ms + EPS) * n2_ref[...]
    row = lax.broadcasted_iota(jnp.int32, (hr + tm, 1), 0)
    live = jnp.logical_or(row >= hr, pl.program_id(1) > 0)
    hext = jnp.where(live, h2, 0.0).astype(BF16)
    o_ref[0] = x1[hr:, :]

    def conv(u_ref, c0, scale=1.0):
        cw = cw_ref[:, c0:c0 + tn] * scale
        y = cb_ref[:, c0:c0 + tn] * scale + (
            cw[FFN_CONV_WIDTH - 1:FFN_CONV_WIDTH, :] * u_ref[hr:hr + tm, :])
        for d in range(1, FFN_CONV_WIDTH):
            tap = FFN_CONV_WIDTH - 1 - d
            y = y + cw[tap:tap + 1, :] * u_ref[hr - d:hr - d + tm, :]
        return y

    def up(j, buf):
        g0, v0 = j * tn, D_FF + j * tn
        buf[0][...] = jnp.dot(hext, wu_ref[:, g0:g0 + tn], preferred_element_type=F32)
        buf[1][...] = jnp.dot(hext, wu_ref[:, v0:v0 + tn], preferred_element_type=F32)

    def down(j, buf):
        g = conv(buf[0], j * tn)
        half_val = conv(buf[1], D_FF + j * tn, 0.5)
        act = _gelu_gate(g, half_val).astype(BF16)
        o_ref[0] += jnp.dot(act, wd_ref[j * tn:(j + 1) * tn, :], preferred_element_type=F32)

    bufs = ((uga_ref, uva_ref), (ugb_ref, uvb_ref))
    up(0, bufs[0])
    for j in range(nj):
        if j + 1 < nj:
            up(j + 1, bufs[(j + 1) % 2])
        down(j, bufs[j % 2])


def _mlp(attn, y, x, w_out, n2, w_up, cw, cb, w_down):
    b, s, d = x.shape
    tm, tn = FFN_ROWS, FFN_COLS
    hr = SUBLANES_BF16
    row_blk = lambda w: pl.BlockSpec((1, tm, w), lambda bi, i: (bi, i, 0))
    hist_blk = lambda w: pl.BlockSpec(
        (1, hr, w), lambda bi, i: (bi, jnp.maximum(i * (tm // hr) - 1, 0), 0))
    assert tm == ATTN_BLOCK
    nblk = s // tm
    nh = nblk // 2
    fold = lambda i: jnp.where(i < nh, i, nblk - 1 - i)
    attn_blk = pl.BlockSpec((1, 1, tm, ATTN_WIDTH), lambda bi, i: (bi, i // nh, fold(i), 0))

    def attn_hist_map(bi, i):
        ip = jnp.maximum(i - 1, 0)
        return (bi, ip // nh, fold(ip) * (tm // hr) + (tm // hr - 1), 0)

    attn_hist_blk = pl.BlockSpec((1, 1, hr, ATTN_WIDTH), attn_hist_map)
    resident = lambda shape: pl.BlockSpec(shape, lambda bi, i: (0,) * len(shape),
                                          pipeline_mode=pl.Buffered(1))
    return pl.pallas_call(
        _mlp_kernel,
        grid=(b, s // tm),
        in_specs=[attn_blk, attn_hist_blk, row_blk(RNN_WIDTH),
                  hist_blk(RNN_WIDTH), row_blk(d), hist_blk(d),
                  resident((d, d)), resident((1, d)), resident((d, 2 * D_FF)),
                  resident((FFN_CONV_WIDTH, 2 * D_FF)), resident((1, 2 * D_FF)),
                  resident((D_FF, d))],
        out_specs=row_blk(d),
        out_shape=jax.ShapeDtypeStruct((b, s, d), F32),
        scratch_shapes=[pltpu.VMEM((hr + tm, tn), F32)] * 4,
        compiler_params=pltpu.CompilerParams(
            dimension_semantics=("arbitrary", "arbitrary"),
            vmem_limit_bytes=VMEM_LIMIT_BYTES),
        name="mlp",
    )(attn, attn, y, y, x, x, w_out, n2, w_up, cw, cb, w_down)


def _rope_tables(seq):
    pos = jnp.arange(seq, dtype=F32)
    inv_freq = ROPE_THETA ** (-(jnp.arange(0, ROT_DIM, 2, dtype=F32) / ROT_DIM))
    ang = pos[:, None] * inv_freq[None, :]
    cos, sin = jnp.cos(ang), jnp.sin(ang)
    half = ROT_DIM // 2
    pad = QK_HEAD_DIM - ROT_DIM
    one = jnp.ones((seq, pad), F32)
    zero_h = jnp.zeros((seq, half), F32)
    zero_p = jnp.zeros((seq, pad), F32)
    rc = jnp.concatenate([cos, cos, one], axis=1)
    rs1 = jnp.concatenate([-sin, zero_h, zero_p], axis=1)
    rs2 = jnp.concatenate([zero_h, sin, zero_p], axis=1)
    return (cos.T, sin.T) + tuple(jnp.tile(t, (1, 2)) for t in (rc, rs1, rs2))


def _block_diag(w):
    h, i, j = w.shape
    eye = jnp.eye(h, dtype=w.dtype)
    return (eye[:, None, :, None] * w[:, :, None, :]).reshape(h * i, h * j)


def kernel(x, norm1_w, w_in, q_norm_w, k_norm_w, lambda_q1, lambda_k1, lambda_q2, lambda_k2,
           subln_w, conv_rnn_w, conv_rnn_b, w_gate_a, b_gate_a, w_gate_x, b_gate_x, lru_lambda,
           rnn_norm_w, w_out, norm2_w, w_up, conv_ffn_w, conv_ffn_b, w_down):
    b, s, d = x.shape
    ct, st, rc, rs1, rs2 = _rope_tables(s)
    row = lambda v: v.reshape(1, -1)
    for l in range(DEPTH):
        lambda_init = 0.8 - 0.6 * math.exp(-0.3 * l)
        w = w_in[l].astype(BF16)
        v0 = 2 * QK_WIDTH
        wt = jnp.concatenate([w[:, :QK_WIDTH], w[:, v0:v0 + ATTN_WIDTH]], axis=1).T
        w_rest = jnp.concatenate([w[:, QK_WIDTH:v0], w[:, v0 + ATTN_WIDTH:]], axis=1)
        qt, k, vt, xr, gr = _in_proj(
            x, row(norm1_w[l]), wt, w_rest, q_norm_w[l].reshape(-1, 1),
            row(jnp.tile(k_norm_w[l], 2)), ct, st, rc, rs1, rs2)
        attn = _attention(qt, k, vt, row(lambda_q1[l]), row(lambda_k1[l]), row(lambda_q2[l]),
                          row(lambda_k2[l]), row(subln_w[l]), lambda_init)
        y = _rglru(xr, gr, conv_rnn_w[l], row(conv_rnn_b[l]),
                   _block_diag(w_gate_a[l]).astype(BF16), row(b_gate_a[l]),
                   _block_diag(w_gate_x[l]).astype(BF16), row(b_gate_x[l]),
                   row(lru_lambda[l]), row(rnn_norm_w[l]))
        x = _mlp(attn, y, x, w_out[l].astype(BF16), row(norm2_w[l]), w_up[l].astype(BF16),
                 conv_ffn_w[l], row(conv_ffn_b[l]), w_down[l].astype(BF16))
    return x
```

```python
import functools
import math

import jax
import jax.numpy as jnp
from jax import lax
from jax.experimental import pallas as pl
from jax.experimental.pallas import tpu as pltpu

D_MODEL = 1024
DEPTH = 1
ATTN_HEADS = 4
QK_HEAD_DIM = 64
V_HEAD_DIM = 2 * QK_HEAD_DIM
ATTN_WIDTH = ATTN_HEADS * V_HEAD_DIM
QK_WIDTH = ATTN_HEADS * 2 * QK_HEAD_DIM
ROT_DIM = QK_HEAD_DIM // 4
ROPE_THETA = 500000.0
RNN_WIDTH = D_MODEL - ATTN_WIDTH
RNN_HEADS = 8
RNN_BLOCK = RNN_WIDTH // RNN_HEADS
RNN_CONV_WIDTH = 4
LRU_C = 8.0
D_FF = 3 * D_MODEL
FFN_CONV_WIDTH = 3
EPS = 1e-6

LANES = 128
SUBLANES_F32 = 8
SUBLANES_BF16 = 16
VMEM_LIMIT_BYTES = 56 * 1024 * 1024

PROJ_ROWS = 512
ATTN_BLOCK = 512
ATTN_GROUP = 4
RNN_ROWS = 512
FFN_ROWS = 512

LOG2E = math.log2(math.e)
F32 = jnp.float32
BF16 = jnp.bfloat16


def _gelu_gate(x, half_v):
    c = math.sqrt(2.0 / math.pi)
    t = jnp.tanh(x * (c + (c * 0.044715) * (x * x)))
    h = x * half_v
    return h + h * t


def _in_proj_kernel(x_ref, n1_ref, wt_ref, w_ref, qw_ref, kw_ref, ct_ref, st_ref, rc_ref,
                    rs1_ref, rs2_ref, qt_ref, k_ref, vt_ref, xr_ref, gr_ref):
    x = x_ref[0]
    ms = jnp.mean(x * x, axis=-1, keepdims=True)
    h = (x * lax.rsqrt(ms + EPS) * n1_ref[...]).astype(BF16)

    half = ROT_DIM // 2
    lane = lax.broadcasted_iota(jnp.int32, (1, LANES), 1)
    lo = lane < QK_HEAD_DIM
    rc, rs1, rs2 = rc_ref[...], rs1_ref[...], rs2_ref[...]
    pk = jnp.dot(h, w_ref[:, 0:QK_WIDTH], preferred_element_type=F32)
    kw = kw_ref[...]
    for hd in range(ATTN_HEADS):
        y = pk[:, LANES * hd:LANES * (hd + 1)]
        sq = y * y
        s_all = jnp.sum(sq, axis=-1, keepdims=True)
        s_lo = jnp.sum(jnp.where(lo, sq, 0.0), axis=-1, keepdims=True)
        msq = jnp.where(lo, s_lo, s_all - s_lo) * (1.0 / QK_HEAD_DIM)
        y = y * lax.rsqrt(msq + EPS) * kw
        up = pltpu.roll(y, LANES - half, axis=1)
        dn = pltpu.roll(y, half, axis=1)
        k_ref[0, :, LANES * hd:LANES * (hd + 1)] = (y * rc + up * rs1 + dn * rs2).astype(BF16)
    c = QK_WIDTH
    xr_ref[0] = jnp.dot(h, w_ref[:, c:c + RNN_WIDTH], preferred_element_type=F32)
    c += RNN_WIDTH
    gr_ref[0] = jnp.dot(h, w_ref[:, c:c + RNN_WIDTH], preferred_element_type=F32)

    nt = (((1,), (1,)), ((), ()))
    vt_ref[0] = lax.dot_general(wt_ref[QK_WIDTH:, :], h, nt,
                                preferred_element_type=F32).astype(BF16)
    pq = lax.dot_general(wt_ref[0:QK_WIDTH, :], h, nt, preferred_element_type=F32)
    ct, st = ct_ref[...], st_ref[...]
    q_scale = LOG2E * QK_HEAD_DIM ** -0.5
    for blk in range(QK_WIDTH // QK_HEAD_DIM):
        r0 = blk * QK_HEAD_DIM
        z = pq[r0:r0 + QK_HEAD_DIM, :]
        msq = jnp.mean(z * z, axis=0, keepdims=True)
        z = z * lax.rsqrt(msq + EPS) * qw_ref[...]
        x1, x2 = z[0:half, :], z[half:ROT_DIM, :]
        z = jnp.concatenate([x1 * ct - x2 * st, x2 * ct + x1 * st, z[ROT_DIM:, :]], axis=0)
        qt_ref[0, r0:r0 + QK_HEAD_DIM, :] = (z * q_scale).astype(BF16)


def _in_proj(x, n1, wt, w_rest, qw_col, kw2, ct, st, rc, rs1, rs2):
    b, s, d = x.shape
    tm = PROJ_ROWS
    row_blk = lambda w: pl.BlockSpec((1, tm, w), lambda bi, i: (bi, i, 0))
    col_blk = lambda w: pl.BlockSpec((1, w, tm), lambda bi, i: (bi, 0, i))
    const = lambda shape: pl.BlockSpec(shape, lambda bi, i: (0,) * len(shape))
    rope_blk = pl.BlockSpec((tm, LANES), lambda bi, i: (i, 0))
    rope_t_blk = pl.BlockSpec((ROT_DIM // 2, tm), lambda bi, i: (0, i))
    return pl.pallas_call(
        _in_proj_kernel,
        grid=(b, s // tm),
        in_specs=[row_blk(d), const((1, d)), const(wt.shape), const(w_rest.shape),
                  const((QK_HEAD_DIM, 1)), const((1, LANES)), rope_t_blk, rope_t_blk,
                  rope_blk, rope_blk, rope_blk],
        out_specs=[col_blk(QK_WIDTH), row_blk(QK_WIDTH), col_blk(ATTN_WIDTH),
                   row_blk(RNN_WIDTH), row_blk(RNN_WIDTH)],
        out_shape=[jax.ShapeDtypeStruct((b, QK_WIDTH, s), BF16),
                   jax.ShapeDtypeStruct((b, s, QK_WIDTH), BF16),
                   jax.ShapeDtypeStruct((b, ATTN_WIDTH, s), BF16),
                   jax.ShapeDtypeStruct((b, s, RNN_WIDTH), F32),
                   jax.ShapeDtypeStruct((b, s, RNN_WIDTH), F32)],
        compiler_params=pltpu.CompilerParams(
            dimension_semantics=("arbitrary", "arbitrary"),
            vmem_limit_bytes=VMEM_LIMIT_BYTES),
        name="in_proj",
    )(x, n1, wt, w_rest, qw_col, kw2, ct, st, rc, rs1, rs2)


def _attn_kernel(qlo_ref, qhi_ref, k_ref, vt_ref, lq1_ref, lk1_ref, lq2_ref, lk2_ref, sw_ref,
                 o_ref, qz_ref, m_ref, acc_ref, sa_ref, mxa_ref, sb_ref,
                 mxb_ref, *, lambda_init, n_blocks):
    t = ATTN_BLOCK
    lo = pl.program_id(2)
    hi = n_blocks - 1 - lo
    dim = lax.broadcasted_iota(jnp.int32, (LANES, 1), 0)
    for w, q_ref in enumerate((qlo_ref, qhi_ref)):
        qt = q_ref[0]
        zero = jnp.zeros_like(qt)
        qz_ref[w, :, 0:t] = jnp.where(dim < QK_HEAD_DIM, qt, zero)
        qz_ref[w, :, t:2 * t] = jnp.where(dim >= QK_HEAD_DIM, qt, zero)

    m_ref[...] = jnp.full(m_ref.shape, -1e30, F32)
    acc_ref[...] = jnp.zeros(acc_ref.shape, F32)

    def scores(buf, w, j, masked):
        s_ref, mx_ref = buf
        r0 = pl.multiple_of(j * t, t)
        kb = k_ref[0, pl.ds(r0, t), :]
        s = jnp.dot(kb, qz_ref[w], preferred_element_type=F32)
        if masked:
            key = lax.broadcasted_iota(jnp.int32, (t, 2 * t), 0)
            qry = lax.broadcasted_iota(jnp.int32, (t, 2 * t), 1) & (t - 1)
            s = jnp.where(key <= qry, s, -jnp.inf)
        s_ref[...] = s
        mx_ref[...] = jnp.max(s, axis=0, keepdims=True)

    def accumulate(buf, w, j):
        s_ref, mx_ref = buf
        r0 = pl.multiple_of(j * t, t)
        vtb = jnp.concatenate([vt_ref[0, :, pl.ds(r0, t)],
                               jnp.ones((SUBLANES_BF16, t), BF16)], axis=0)
        m_prev = m_ref[w]
        m_new = jnp.maximum(m_prev, mx_ref[...])
        alpha = jnp.exp2(m_prev - m_new)
        e = jnp.exp2(s_ref[...] - m_new)
        acc_ref[w] = alpha * acc_ref[w] + jnp.dot(vtb, e.astype(BF16),
                                                  preferred_element_type=F32)
        m_ref[w] = m_new

    n_items = n_blocks + 1
    group = ATTN_GROUP
    assert (n_items - 1) % group == 0

    def item(n):
        if isinstance(n, int) and n < 2:
            return (n, (lo, hi)[n])
        is_hi = (n - 2) >= lo
        return (is_hi.astype(jnp.int32), jnp.where(is_hi, n - 2 - lo, n - 2))

    bufs = ((sa_ref, mxa_ref), (sb_ref, mxb_ref))

    def run_group(first, masked_until):
        for r in range(group):
            n = first + r
            scores(bufs[(r + 1) % 2], *item(n + 1),
                   isinstance(n, int) and n + 1 < masked_until)
            accumulate(bufs[r % 2], *item(n))

    scores(bufs[0], *item(0), True)
    run_group(0, 2)

    def trip(i, carry):
        run_group(i * group, 0)
        return carry

    lax.fori_loop(1, (n_items - 1) // group, trip, 0)
    accumulate(bufs[0], *item(n_items - 1))

    lam = (jnp.exp(jnp.sum(lq1_ref[...] * lk1_ref[...], axis=-1, keepdims=True))
           - jnp.exp(jnp.sum(lq2_ref[...] * lk2_ref[...], axis=-1, keepdims=True))
           + lambda_init)
    dv = V_HEAD_DIM
    for w in range(2):
        l0 = acc_ref[w, dv:dv + 1, 0:t]
        l1 = acc_ref[w, dv:dv + 1, t:2 * t]
        o = acc_ref[w, 0:dv, 0:t] * (1.0 / l0) - acc_ref[w, 0:dv, t:2 * t] * (lam / l1)
        ms = jnp.mean(o * o, axis=0, keepdims=True)
        o = o * (lax.rsqrt(ms + EPS) * (1.0 - lambda_init))
        o_ref[0, w] = (o.T * sw_ref[...]).astype(BF16)


def _attention(qt, k, vt, lq1, lk1, lq2, lk2, sw, lambda_init):
    b, s, _ = k.shape
    t = ATTN_BLOCK
    nb = s // t
    half = nb // 2
    vec = lambda n: pl.BlockSpec((1, n), lambda bi, h, p: (0, 0))
    return pl.pallas_call(
        functools.partial(_attn_kernel, lambda_init=lambda_init, n_blocks=nb),
        grid=(b, ATTN_HEADS, half),
        in_specs=[pl.BlockSpec((1, LANES, t), lambda bi, h, p: (bi, h, p)),
                  pl.BlockSpec((1, LANES, t), lambda bi, h, p: (bi, h, nb - 1 - p)),
                  pl.BlockSpec((1, s, LANES), lambda bi, h, p: (bi, 0, h)),
                  pl.BlockSpec((1, LANES, s), lambda bi, h, p: (bi, h, 0)),
                  vec(QK_HEAD_DIM), vec(QK_HEAD_DIM), vec(QK_HEAD_DIM), vec(QK_HEAD_DIM),
                  vec(V_HEAD_DIM)],
        out_specs=pl.BlockSpec((1, 2, t, LANES), lambda bi, h, p: (bi, 0, p, h)),
        out_shape=jax.ShapeDtypeStruct((b, 2, s // 2, ATTN_WIDTH), BF16),
        scratch_shapes=[pltpu.VMEM((2, LANES, 2 * t), BF16),
                        pltpu.VMEM((2, 1, 2 * t), F32),
                        pltpu.VMEM((2, V_HEAD_DIM + SUBLANES_BF16, 2 * t), F32),
                        pltpu.VMEM((t, 2 * t), F32), pltpu.VMEM((1, 2 * t), F32),
                        pltpu.VMEM((t, 2 * t), F32), pltpu.VMEM((1, 2 * t), F32)],
        compiler_params=pltpu.CompilerParams(
            dimension_semantics=("arbitrary", "arbitrary", "arbitrary"),
            vmem_limit_bytes=VMEM_LIMIT_BYTES),
        name="diff_attn",
    )(qt, qt, k, vt, lq1, lk1, lq2, lk2, sw)


def _rglru_kernel(xr_ref, gr_ref, cw_ref, cb_ref, wa_ref, ba_ref, wx_ref, bx_ref, lam_ref,
                  nw_ref, y_ref, ext_ref, a_ref, b_ref, hc_ref):
    ts = RNN_ROWS
    halo = SUBLANES_F32

    @pl.when(pl.program_id(1) == 0)
    def _():
        ext_ref[0:halo, :] = jnp.zeros((halo, RNN_WIDTH), F32)
        hc_ref[...] = jnp.zeros((halo, RNN_WIDTH), F32)

    x = xr_ref[0]
    ext_ref[halo:halo + ts, :] = x
    xc = cb_ref[...] + cw_ref[RNN_CONV_WIDTH - 1:RNN_CONV_WIDTH, :] * x
    for d in range(1, RNN_CONV_WIDTH):
        tap = RNN_CONV_WIDTH - 1 - d
        xc = xc + cw_ref[tap:tap + 1, :] * ext_ref[halo - d:halo - d + ts, :]
    ext_ref[0:halo, :] = x[ts - halo:ts, :]

    xb = xc.astype(BF16)
    r = jax.nn.sigmoid(jnp.dot(xb, wa_ref[...], preferred_element_type=F32) + ba_ref[...])
    ig = jax.nn.sigmoid(jnp.dot(xb, wx_ref[...], preferred_element_type=F32) + bx_ref[...])
    z = -lam_ref[...]
    softplus = jnp.maximum(z, 0.0) + jnp.log1p(jnp.exp(-jnp.abs(z)))
    log_a = r * (-LRU_C * softplus)
    a = jnp.exp(log_a)
    w = 1.0 - a * a
    mult = jnp.where(w > 0.0, w * lax.rsqrt(w), 0.0)
    u = mult * ig * xc

    tiles = (ts // SUBLANES_F32, SUBLANES_F32, RNN_WIDTH)
    a = a.reshape(tiles)
    u = u.reshape(tiles)
    sub = lax.broadcasted_iota(jnp.int32, (1, SUBLANES_F32, 1), 1)
    d = 1
    while d < SUBLANES_F32:
        keep = sub >= d
        a_prev = jnp.where(keep, pltpu.roll(a, d, axis=1), 1.0)
        u_prev = jnp.where(keep, pltpu.roll(u, d, axis=1), 0.0)
        u = a * u_prev + u
        a = a * a_prev
        d *= 2
    a_ref[...] = a.reshape(ts, RNN_WIDTH)
    b_ref[...] = u.reshape(ts, RNN_WIDTH)

    def tile(j, h_prev):
        r0 = pl.multiple_of(j * SUBLANES_F32, SUBLANES_F32)
        h = b_ref[pl.ds(r0, SUBLANES_F32), :] + a_ref[pl.ds(r0, SUBLANES_F32), :] * h_prev
        b_ref[pl.ds(r0, SUBLANES_F32), :] = h
        return jnp.broadcast_to(h[SUBLANES_F32 - 1:SUBLANES_F32, :], h.shape)

    hc_ref[...] = lax.fori_loop(0, ts // SUBLANES_F32, tile, hc_ref[...], unroll=8)

    y = _gelu_gate(gr_ref[0], 0.5 * b_ref[...])
    ms = jnp.mean(y * y, axis=-1, keepdims=True)
    y_ref[0] = (y * lax.rsqrt(ms + EPS) * nw_ref[...]).astype(BF16)


def _rglru(xr, gr, cw, cb, wa, ba, wx, bx, lam, nw):
    b, s, c = xr.shape
    ts = RNN_ROWS
    row_blk = pl.BlockSpec((1, ts, c), lambda bi, i: (bi, i, 0))
    const = lambda shape: pl.BlockSpec(shape, lambda bi, i: (0,) * len(shape))
    return pl.pallas_call(
        _rglru_kernel,
        grid=(b, s // ts),
        in_specs=[row_blk, row_blk, const((RNN_CONV_WIDTH, c)), const((1, c)),
                  const((c, c)), const((1, c)), const((c, c)), const((1, c)),
                  const((1, c)), const((1, c))],
        out_specs=row_blk,
        out_shape=jax.ShapeDtypeStruct((b, s, c), BF16),
        scratch_shapes=[pltpu.VMEM((ts + SUBLANES_F32, c), F32),
                        pltpu.VMEM((ts, c), F32), pltpu.VMEM((ts, c), F32),
                        pltpu.VMEM((SUBLANES_F32, c), F32)],
        compiler_params=pltpu.CompilerParams(
            dimension_semantics=("arbitrary", "arbitrary"),
            vmem_limit_bytes=VMEM_LIMIT_BYTES),
        name="rglru",
    )(xr, gr, cw, cb, wa, ba, wx, bx, lam, nw)


def _mlp_kernel(attn_ref, attn_h_ref, y_ref, y_h_ref, x_ref, x_h_ref, wo_ref, n2_ref,
                wu_ref, cw_ref, cb_ref, wd_ref, o_ref, ug_ref, uv_ref):
    tm = FFN_ROWS
    hr = SUBLANES_BF16

    attn = jnp.concatenate([attn_h_ref[0, 0], attn_ref[0, 0]], axis=0)
    yrnn = jnp.concatenate([y_h_ref[0], y_ref[0]], axis=0)
    xin = jnp.concatenate([x_h_ref[0], x_ref[0]], axis=0)
    x1 = xin + jnp.dot(attn, wo_ref[0:ATTN_WIDTH, :], preferred_element_type=F32)
    x1 = x1 + jnp.dot(yrnn, wo_ref[ATTN_WIDTH:, :], preferred_element_type=F32)
    ms = jnp.mean(x1 * x1, axis=-1, keepdims=True)
    h2 = x1 * lax.rsqrt(ms + EPS) * n2_ref[...]
    row = lax.broadcasted_iota(jnp.int32, (hr + tm, 1), 0)
    live = jnp.logical_or(row >= hr, pl.program_id(1) > 0)
    hext = jnp.where(live, h2, 0.0).astype(BF16)

    def conv(u_ref, c0, scale=1.0):
        cw = cw_ref[:, c0:c0 + D_FF] * scale
        y = cb_ref[:, c0:c0 + D_FF] * scale + (
            cw[FFN_CONV_WIDTH - 1:FFN_CONV_WIDTH, :] * u_ref[hr:hr + tm, :])
        for d in range(1, FFN_CONV_WIDTH):
            tap = FFN_CONV_WIDTH - 1 - d
            y = y + cw[tap:tap + 1, :] * u_ref[hr - d:hr - d + tm, :]
        return y

    ug_ref[...] = jnp.dot(hext, wu_ref[:, 0:D_FF], preferred_element_type=F32)
    uv_ref[...] = jnp.dot(hext, wu_ref[:, D_FF:], preferred_element_type=F32)
    act = _gelu_gate(conv(ug_ref, 0), conv(uv_ref, D_FF, 0.5)).astype(BF16)
    o_ref[0] = x1[hr:, :] + jnp.dot(act, wd_ref[...], preferred_element_type=F32)


def _mlp(attn, y, x, w_out, n2, w_up, cw, cb, w_down):
    b, s, d = x.shape
    tm = FFN_ROWS
    hr = SUBLANES_BF16
    row_blk = lambda w: pl.BlockSpec((1, tm, w), lambda bi, i: (bi, i, 0))
    hist_blk = lambda w: pl.BlockSpec(
        (1, hr, w), lambda bi, i: (bi, jnp.maximum(i * (tm // hr) - 1, 0), 0))
    ratio = ATTN_BLOCK // tm
    nq = s // ATTN_BLOCK
    nh = nq // 2

    def fold(i):
        qb = i // ratio
        return qb // nh, jnp.where(qb < nh, qb, nq - 1 - qb) * ratio + i % ratio

    attn_blk = pl.BlockSpec((1, 1, tm, ATTN_WIDTH), lambda bi, i: (bi, *fold(i), 0))

    def attn_hist_map(bi, i):
        half, blk = fold(jnp.maximum(i - 1, 0))
        return (bi, half, blk * (tm // hr) + (tm // hr - 1), 0)

    attn_hist_blk = pl.BlockSpec((1, 1, hr, ATTN_WIDTH), attn_hist_map)
    resident = lambda shape: pl.BlockSpec(shape, lambda bi, i: (0,) * len(shape),
                                          pipeline_mode=pl.Buffered(1))
    return pl.pallas_call(
        _mlp_kernel,
        grid=(b, s // tm),
        in_specs=[attn_blk, attn_hist_blk, row_blk(RNN_WIDTH),
                  hist_blk(RNN_WIDTH), row_blk(d), hist_blk(d),
                  resident((d, d)), resident((1, d)), resident((d, 2 * D_FF)),
                  resident((FFN_CONV_WIDTH, 2 * D_FF)), resident((1, 2 * D_FF)),
                  resident((D_FF, d))],
        out_specs=row_blk(d),
        out_shape=jax.ShapeDtypeStruct((b, s, d), F32),
        scratch_shapes=[pltpu.VMEM((hr + tm, D_FF), F32)] * 2,
        compiler_params=pltpu.CompilerParams(
            dimension_semantics=("arbitrary", "arbitrary"),
            vmem_limit_bytes=VMEM_LIMIT_BYTES),
        name="mlp",
    )(attn, attn, y, y, x, x, w_out, n2, w_up, cw, cb, w_down)


def _rope_tables(seq):
    pos = jnp.arange(seq, dtype=F32)
    inv_freq = ROPE_THETA ** (-(jnp.arange(0, ROT_DIM, 2, dtype=F32) / ROT_DIM))
    ang = pos[:, None] * inv_freq[None, :]
    cos, sin = jnp.cos(ang), jnp.sin(ang)
    half = ROT_DIM // 2
    pad = QK_HEAD_DIM - ROT_DIM
    one = jnp.ones((seq, pad), F32)
    zero_h = jnp.zeros((seq, half), F32)
    zero_p = jnp.zeros((seq, pad), F32)
    rc = jnp.concatenate([cos, cos, one], axis=1)
    rs1 = jnp.concatenate([-sin, zero_h, zero_p], axis=1)
    rs2 = jnp.concatenate([zero_h, sin, zero_p], axis=1)
    return (cos.T, sin.T) + tuple(jnp.tile(t, (1, 2)) for t in (rc, rs1, rs2))


def _block_diag(w):
    h, i, j = w.shape
    eye = jnp.eye(h, dtype=w.dtype)
    return (eye[:, None, :, None] * w[:, :, None, :]).reshape(h * i, h * j)


def kernel(x, norm1_w, w_in, q_norm_w, k_norm_w, lambda_q1, lambda_k1, lambda_q2, lambda_k2,
           subln_w, conv_rnn_w, conv_rnn_b, w_gate_a, b_gate_a, w_gate_x, b_gate_x, lru_lambda,
           rnn_norm_w, w_out, norm2_w, w_up, conv_ffn_w, conv_ffn_b, w_down):
    b, s, d = x.shape
    ct, st, rc, rs1, rs2 = _rope_tables(s)
    row = lambda v: v.reshape(1, -1)
    for l in range(DEPTH):
        lambda_init = 0.8 - 0.6 * math.exp(-0.3 * l)
        w = w_in[l].astype(BF16)
        v0 = 2 * QK_WIDTH
        wt = jnp.concatenate([w[:, :QK_WIDTH], w[:, v0:v0 + ATTN_WIDTH]], axis=1).T
        w_rest = jnp.concatenate([w[:, QK_WIDTH:v0], w[:, v0 + ATTN_WIDTH:]], axis=1)
        qt, k, vt, xr, gr = _in_proj(
            x, row(norm1_w[l]), wt, w_rest, q_norm_w[l].reshape(-1, 1),
            row(jnp.tile(k_norm_w[l], 2)), ct, st, rc, rs1, rs2)
        attn = _attention(qt, k, vt, row(lambda_q1[l]), row(lambda_k1[l]), row(lambda_q2[l]),
                          row(lambda_k2[l]), row(subln_w[l]), lambda_init)
        y = _rglru(xr, gr, conv_rnn_w[l], row(conv_rnn_b[l]),
                   _block_diag(w_gate_a[l]).astype(BF16), row(b_gate_a[l]),
                   _block_diag(w_gate_x[l]).astype(BF16), row(b_gate_x[l]),
                   row(lru_lambda[l]), row(rnn_norm_w[l]))
        x = _mlp(attn, y, x, w_out[l].astype(BF16), row(norm2_w[l]), w_up[l].astype(BF16),
                 conv_ffn_w[l], row(conv_ffn_b[l]), w_down[l].astype(BF16))
    return x
```

```python
import functools
import math

import jax
import jax.numpy as jnp
import numpy as np
from jax import lax
from jax.experimental import pallas as pl
from jax.experimental.pallas import tpu as pltpu

D_MODEL = 1024
DEPTH = 1
ATTN_HEADS = 4
QK_HEAD_DIM = 64
V_HEAD_DIM = 2 * QK_HEAD_DIM
ATTN_WIDTH = ATTN_HEADS * V_HEAD_DIM
QK_WIDTH = ATTN_HEADS * 2 * QK_HEAD_DIM
ROT_DIM = QK_HEAD_DIM // 4
ROPE_THETA = 500000.0
RNN_WIDTH = D_MODEL - ATTN_WIDTH
RNN_HEADS = 8
RNN_BLOCK = RNN_WIDTH // RNN_HEADS
RNN_CONV_WIDTH = 4
LRU_C = 8.0
D_FF = 3 * D_MODEL
FFN_CONV_WIDTH = 3
EPS = 1e-6

LANES = 128
SUBLANES_F32 = 8
SUBLANES_BF16 = 16
VMEM_LIMIT_BYTES = 56 * 1024 * 1024

PROJ_ROWS = 512
ATTN_BLOCK = 512
ATTN_GROUP = 4
RNN_ROWS = 512
FFN_ROWS = 512

LOG2E = math.log2(math.e)
F32 = jnp.float32
BF16 = jnp.bfloat16


def _gelu_gate(x, half_v):
    c = math.sqrt(2.0 / math.pi)
    t = jnp.tanh(x * (c + (c * 0.044715) * (x * x)))
    h = x * half_v
    return h + h * t


def _in_proj_kernel(x_ref, n1_ref, wt_ref, w_ref, qw_ref, kw_ref, ct_ref, st_ref, rc_ref,
                    rs1_ref, rs2_ref, wo_ref, wu_ref, wd_ref,
                    qt_ref, k_ref, vt_ref, xr_ref, gr_ref, wo16_ref, wu16_ref, wd16_ref):
    wo16_ref[...] = wo_ref[...].astype(BF16)
    wu16_ref[...] = wu_ref[...].astype(BF16)
    wd16_ref[...] = wd_ref[...].astype(BF16)

    x = x_ref[0]
    ms = jnp.mean(x * x, axis=-1, keepdims=True)
    h = (x * lax.rsqrt(ms + EPS) * n1_ref[...]).astype(BF16)

    half = ROT_DIM // 2
    lane = lax.broadcasted_iota(jnp.int32, (1, LANES), 1)
    lo = lane < QK_HEAD_DIM
    rc, rs1, rs2 = rc_ref[...], rs1_ref[...], rs2_ref[...]
    pk = jnp.dot(h, w_ref[:, 0:QK_WIDTH], preferred_element_type=F32)
    kw = kw_ref[...]
    for hd in range(ATTN_HEADS):
        y = pk[:, LANES * hd:LANES * (hd + 1)]
        sq = y * y
        s_all = jnp.sum(sq, axis=-1, keepdims=True)
        s_lo = jnp.sum(jnp.where(lo, sq, 0.0), axis=-1, keepdims=True)
        msq = jnp.where(lo, s_lo, s_all - s_lo) * (1.0 / QK_HEAD_DIM)
        y = y * lax.rsqrt(msq + EPS) * kw
        up = pltpu.roll(y, LANES - half, axis=1)
        dn = pltpu.roll(y, half, axis=1)
        k_ref[0, :, LANES * hd:LANES * (hd + 1)] = (y * rc + up * rs1 + dn * rs2).astype(BF16)
    c = QK_WIDTH
    xr_ref[0] = jnp.dot(h, w_ref[:, c:c + RNN_WIDTH], preferred_element_type=F32)
    c += RNN_WIDTH
    gr_ref[0] = jnp.dot(h, w_ref[:, c:c + RNN_WIDTH], preferred_element_type=F32)

    nt = (((1,), (1,)), ((), ()))
    vt_ref[0] = lax.dot_general(wt_ref[QK_WIDTH:, :], h, nt,
                                preferred_element_type=F32).astype(BF16)
    pq = lax.dot_general(wt_ref[0:QK_WIDTH, :], h, nt, preferred_element_type=F32)
    ct, st = ct_ref[...], st_ref[...]
    q_scale = LOG2E * QK_HEAD_DIM ** -0.5
    for blk in range(QK_WIDTH // QK_HEAD_DIM):
        r0 = blk * QK_HEAD_DIM
        z = pq[r0:r0 + QK_HEAD_DIM, :]
        msq = jnp.mean(z * z, axis=0, keepdims=True)
        z = z * lax.rsqrt(msq + EPS) * qw_ref[...]
        x1, x2 = z[0:half, :], z[half:ROT_DIM, :]
        z = jnp.concatenate([x1 * ct - x2 * st, x2 * ct + x1 * st, z[ROT_DIM:, :]], axis=0)
        qt_ref[0, r0:r0 + QK_HEAD_DIM, :] = (z * q_scale).astype(BF16)


def _in_proj(x, n1, wt, w_rest, qw_col, kw2, ct, st, rc, rs1, rs2, later_weights):
    b, s, d = x.shape
    tm = PROJ_ROWS
    n_steps = b * (s // tm)

    def slab(w):
        rows, cols = w.shape
        assert rows % (n_steps * SUBLANES_BF16) == 0
        return pl.BlockSpec((rows // n_steps, cols), lambda bi, i: (bi * (s // tm) + i, 0))

    row_blk = lambda w: pl.BlockSpec((1, tm, w), lambda bi, i: (bi, i, 0))
    col_blk = lambda w: pl.BlockSpec((1, w, tm), lambda bi, i: (bi, 0, i))
    const = lambda shape: pl.BlockSpec(shape, lambda bi, i: (0,) * len(shape))
    rope_blk = pl.BlockSpec((tm, LANES), lambda bi, i: (i, 0))
    rope_t_blk = pl.BlockSpec((ROT_DIM // 2, tm), lambda bi, i: (0, i))
    return pl.pallas_call(
        _in_proj_kernel,
        grid=(b, s // tm),
        in_specs=[row_blk(d), const((1, d)), const(wt.shape), const(w_rest.shape),
                  const((QK_HEAD_DIM, 1)), const((1, LANES)), rope_t_blk, rope_t_blk,
                  rope_blk, rope_blk, rope_blk] + [slab(w) for w in later_weights],
        out_specs=[col_blk(QK_WIDTH), row_blk(QK_WIDTH), col_blk(ATTN_WIDTH),
                   row_blk(RNN_WIDTH), row_blk(RNN_WIDTH)] + [slab(w) for w in later_weights],
        out_shape=[jax.ShapeDtypeStruct((b, QK_WIDTH, s), BF16),
                   jax.ShapeDtypeStruct((b, s, QK_WIDTH), BF16),
                   jax.ShapeDtypeStruct((b, ATTN_WIDTH, s), BF16),
                   jax.ShapeDtypeStruct((b, s, RNN_WIDTH), F32),
                   jax.ShapeDtypeStruct((b, s, RNN_WIDTH), F32)]
        + [jax.ShapeDtypeStruct(w.shape, BF16) for w in later_weights],
        compiler_params=pltpu.CompilerParams(
            dimension_semantics=("arbitrary", "arbitrary"),
            vmem_limit_bytes=VMEM_LIMIT_BYTES),
        name="in_proj",
    )(x, n1, wt, w_rest, qw_col, kw2, ct, st, rc, rs1, rs2, *later_weights)


def _attn_kernel(qlo_ref, qhi_ref, k_ref, vt_ref, lq1_ref, lk1_ref, lq2_ref, lk2_ref, sw_ref,
                 o_ref, qz_ref, m_ref, acc_ref, sa_ref, mxa_ref, sb_ref,
                 mxb_ref, *, lambda_init, n_blocks):
    t = ATTN_BLOCK
    lo = pl.program_id(2)
    hi = n_blocks - 1 - lo
    dim = lax.broadcasted_iota(jnp.int32, (LANES, 1), 0)
    for w, q_ref in enumerate((qlo_ref, qhi_ref)):
        qt = q_ref[0]
        zero = jnp.zeros_like(qt)
        qz_ref[w, :, 0:t] = jnp.where(dim < QK_HEAD_DIM, qt, zero)
        qz_ref[w, :, t:2 * t] = jnp.where(dim >= QK_HEAD_DIM, qt, zero)

    m_ref[...] = jnp.full(m_ref.shape, -1e30, F32)
    acc_ref[...] = jnp.zeros(acc_ref.shape, F32)

    def scores(buf, w, j, masked):
        s_ref, mx_ref = buf
        r0 = pl.multiple_of(j * t, t)
        kb = k_ref[0, pl.ds(r0, t), :]
        s = jnp.dot(kb, qz_ref[w], preferred_element_type=F32)
        if masked:
            key = lax.broadcasted_iota(jnp.int32, (t, 2 * t), 0)
            qry = lax.broadcasted_iota(jnp.int32, (t, 2 * t), 1) & (t - 1)
            s = jnp.where(key <= qry, s, -jnp.inf)
        s_ref[...] = s
        mx_ref[...] = jnp.max(s, axis=0, keepdims=True)

    def accumulate(buf, w, j):
        s_ref, mx_ref = buf
        r0 = pl.multiple_of(j * t, t)
        vtb = jnp.concatenate([vt_ref[0, :, pl.ds(r0, t)],
                               jnp.ones((SUBLANES_BF16, t), BF16)], axis=0)
        m_prev = m_ref[w]
        m_new = jnp.maximum(m_prev, mx_ref[...])
        alpha = jnp.exp2(m_prev - m_new)
        e = jnp.exp2(s_ref[...] - m_new)
        acc_ref[w] = alpha * acc_ref[w] + jnp.dot(vtb, e.astype(BF16),
                                                  preferred_element_type=F32)
        m_ref[w] = m_new

    n_items = n_blocks + 1
    group = ATTN_GROUP
    assert (n_items - 1) % group == 0

    def item(n):
        if isinstance(n, int) and n < 2:
            return (n, (lo, hi)[n])
        is_hi = (n - 2) >= lo
        return (is_hi.astype(jnp.int32), jnp.where(is_hi, n - 2 - lo, n - 2))

    bufs = ((sa_ref, mxa_ref), (sb_ref, mxb_ref))

    def run_group(first, masked_until):
        for r in range(group):
            n = first + r
            scores(bufs[(r + 1) % 2], *item(n + 1),
                   isinstance(n, int) and n + 1 < masked_until)
            accumulate(bufs[r % 2], *item(n))

    scores(bufs[0], *item(0), True)
    run_group(0, 2)

    def trip(i, carry):
        run_group(i * group, 0)
        return carry

    lax.fori_loop(1, (n_items - 1) // group, trip, 0)
    accumulate(bufs[0], *item(n_items - 1))

    lam = (jnp.exp(jnp.sum(lq1_ref[...] * lk1_ref[...], axis=-1, keepdims=True))
           - jnp.exp(jnp.sum(lq2_ref[...] * lk2_ref[...], axis=-1, keepdims=True))
           + lambda_init)
    dv = V_HEAD_DIM
    for w in range(2):
        l0 = acc_ref[w, dv:dv + 1, 0:t]
        l1 = acc_ref[w, dv:dv + 1, t:2 * t]
        o = acc_ref[w, 0:dv, 0:t] * (1.0 / l0) - acc_ref[w, 0:dv, t:2 * t] * (lam / l1)
        ms = jnp.mean(o * o, axis=0, keepdims=True)
        o = o * (lax.rsqrt(ms + EPS) * (1.0 - lambda_init))
        o_ref[0, w] = (o.T * sw_ref[...]).astype(BF16)


def _attention(qt, k, vt, lq1, lk1, lq2, lk2, sw, lambda_init):
    b, s, _ = k.shape
    t = ATTN_BLOCK
    nb = s // t
    half = nb // 2
    vec = lambda n: pl.BlockSpec((1, n), lambda bi, h, p: (0, 0))
    return pl.pallas_call(
        functools.partial(_attn_kernel, lambda_init=lambda_init, n_blocks=nb),
        grid=(b, ATTN_HEADS, half),
        in_specs=[pl.BlockSpec((1, LANES, t), lambda bi, h, p: (bi, h, p)),
                  pl.BlockSpec((1, LANES, t), lambda bi, h, p: (bi, h, nb - 1 - p)),
                  pl.BlockSpec((1, s, LANES), lambda bi, h, p: (bi, 0, h)),
                  pl.BlockSpec((1, LANES, s), lambda bi, h, p: (bi, h, 0)),
                  vec(QK_HEAD_DIM), vec(QK_HEAD_DIM), vec(QK_HEAD_DIM), vec(QK_HEAD_DIM),
                  vec(V_HEAD_DIM)],
        out_specs=pl.BlockSpec((1, 2, t, LANES), lambda bi, h, p: (bi, 0, p, h)),
        out_shape=jax.ShapeDtypeStruct((b, 2, s // 2, ATTN_WIDTH), BF16),
        scratch_shapes=[pltpu.VMEM((2, LANES, 2 * t), BF16),
                        pltpu.VMEM((2, 1, 2 * t), F32),
                        pltpu.VMEM((2, V_HEAD_DIM + SUBLANES_BF16, 2 * t), F32),
                        pltpu.VMEM((t, 2 * t), F32), pltpu.VMEM((1, 2 * t), F32),
                        pltpu.VMEM((t, 2 * t), F32), pltpu.VMEM((1, 2 * t), F32)],
        compiler_params=pltpu.CompilerParams(
            dimension_semantics=("arbitrary", "arbitrary", "arbitrary"),
            vmem_limit_bytes=VMEM_LIMIT_BYTES),
        name="diff_attn",
    )(qt, qt, k, vt, lq1, lk1, lq2, lk2, sw)


def _rglru_kernel(xr_ref, gr_ref, cw_ref, cb_ref, wa_ref, ba_ref, wx_ref, bx_ref, lam_ref,
                  nw_ref, y_ref, ext_ref, a_ref, b_ref, hc_ref):
    ts = RNN_ROWS
    halo = SUBLANES_F32

    @pl.when(pl.program_id(1) == 0)
    def _():
        ext_ref[0:halo, :] = jnp.zeros((halo, RNN_WIDTH), F32)
        hc_ref[...] = jnp.zeros((halo, RNN_WIDTH), F32)

    x = xr_ref[0]
    ext_ref[halo:halo + ts, :] = x
    xc = cb_ref[...] + cw_ref[RNN_CONV_WIDTH - 1:RNN_CONV_WIDTH, :] * x
    for d in range(1, RNN_CONV_WIDTH):
        tap = RNN_CONV_WIDTH - 1 - d
        xc = xc + cw_ref[tap:tap + 1, :] * ext_ref[halo - d:halo - d + ts, :]
    ext_ref[0:halo, :] = x[ts - halo:ts, :]

    xb = xc.astype(BF16)
    r = jax.nn.sigmoid(jnp.dot(xb, wa_ref[...], preferred_element_type=F32) + ba_ref[...])
    ig = jax.nn.sigmoid(jnp.dot(xb, wx_ref[...], preferred_element_type=F32) + bx_ref[...])
    z = -lam_ref[...]
    softplus = jnp.maximum(z, 0.0) + jnp.log1p(jnp.exp(-jnp.abs(z)))
    log_a = r * (-LRU_C * softplus)
    a = jnp.exp(log_a)
    w = 1.0 - a * a
    mult = jnp.where(w > 0.0, w * lax.rsqrt(w), 0.0)
    u = mult * ig * xc

    tiles = (ts // SUBLANES_F32, SUBLANES_F32, RNN_WIDTH)
    a = a.reshape(tiles)
    u = u.reshape(tiles)
    sub = lax.broadcasted_iota(jnp.int32, (1, SUBLANES_F32, 1), 1)
    d = 1
    while d < SUBLANES_F32:
        keep = sub >= d
        a_prev = jnp.where(keep, pltpu.roll(a, d, axis=1), 1.0)
        u_prev = jnp.where(keep, pltpu.roll(u, d, axis=1), 0.0)
        u = a * u_prev + u
        a = a * a_prev
        d *= 2
    a_ref[...] = a.reshape(ts, RNN_WIDTH)
    b_ref[...] = u.reshape(ts, RNN_WIDTH)

    def tile(j, h_prev):
        r0 = pl.multiple_of(j * SUBLANES_F32, SUBLANES_F32)
        h = b_ref[pl.ds(r0, SUBLANES_F32), :] + a_ref[pl.ds(r0, SUBLANES_F32), :] * h_prev
        b_ref[pl.ds(r0, SUBLANES_F32), :] = h
        return jnp.broadcast_to(h[SUBLANES_F32 - 1:SUBLANES_F32, :], h.shape)

    hc_ref[...] = lax.fori_loop(0, ts // SUBLANES_F32, tile, hc_ref[...], unroll=8)

    y = _gelu_gate(gr_ref[0], 0.5 * b_ref[...])
    ms = jnp.mean(y * y, axis=-1, keepdims=True)
    y_ref[0] = (y * lax.rsqrt(ms + EPS) * nw_ref[...]).astype(BF16)


def _rglru(xr, gr, cw, cb, wa, ba, wx, bx, lam, nw):
    b, s, c = xr.shape
    ts = RNN_ROWS
    row_blk = pl.BlockSpec((1, ts, c), lambda bi, i: (bi, i, 0))
    const = lambda shape: pl.BlockSpec(shape, lambda bi, i: (0,) * len(shape))
    return pl.pallas_call(
        _rglru_kernel,
        grid=(b, s // ts),
        in_specs=[row_blk, row_blk, const((RNN_CONV_WIDTH, c)), const((1, c)),
                  const((c, c)), const((1, c)), const((c, c)), const((1, c)),
                  const((1, c)), const((1, c))],
        out_specs=row_blk,
        out_shape=jax.ShapeDtypeStruct((b, s, c), BF16),
        scratch_shapes=[pltpu.VMEM((ts + SUBLANES_F32, c), F32),
                        pltpu.VMEM((ts, c), F32), pltpu.VMEM((ts, c), F32),
                        pltpu.VMEM((SUBLANES_F32, c), F32)],
        compiler_params=pltpu.CompilerParams(
            dimension_semantics=("arbitrary", "arbitrary"),
            vmem_limit_bytes=VMEM_LIMIT_BYTES),
        name="rglru",
    )(xr, gr, cw, cb, wa, ba, wx, bx, lam, nw)


def _mlp_kernel(attn_ref, attn_h_ref, y_ref, y_h_ref, x_ref, x_h_ref, wo_ref, n2_ref,
                wu_ref, cw_ref, cb_ref, wd_ref, o_ref, ug_ref, uv_ref):
    tm = FFN_ROWS
    hr = SUBLANES_BF16

    attn = jnp.concatenate([attn_h_ref[0, 0], attn_ref[0, 0]], axis=0)
    yrnn = jnp.concatenate([y_h_ref[0], y_ref[0]], axis=0)
    xin = jnp.concatenate([x_h_ref[0], x_ref[0]], axis=0)
    x1 = xin + jnp.dot(attn, wo_ref[0:ATTN_WIDTH, :], preferred_element_type=F32)
    x1 = x1 + jnp.dot(yrnn, wo_ref[ATTN_WIDTH:, :], preferred_element_type=F32)
    ms = jnp.mean(x1 * x1, axis=-1, keepdims=True)
    h2 = x1 * lax.rsqrt(ms + EPS) * n2_ref[...]
    row = lax.broadcasted_iota(jnp.int32, (hr + tm, 1), 0)
    live = jnp.logical_or(row >= hr, pl.program_id(1) > 0)
    hext = jnp.where(live, h2, 0.0).astype(BF16)

    def conv(u_ref, c0, scale=1.0):
        cw = cw_ref[:, c0:c0 + D_FF] * scale
        y = cb_ref[:, c0:c0 + D_FF] * scale + (
            cw[FFN_CONV_WIDTH - 1:FFN_CONV_WIDTH, :] * u_ref[hr:hr + tm, :])
        for d in range(1, FFN_CONV_WIDTH):
            tap = FFN_CONV_WIDTH - 1 - d
            y = y + cw[tap:tap + 1, :] * u_ref[hr - d:hr - d + tm, :]
        return y

    ug_ref[...] = jnp.dot(hext, wu_ref[:, 0:D_FF], preferred_element_type=F32)
    uv_ref[...] = jnp.dot(hext, wu_ref[:, D_FF:], preferred_element_type=F32)
    act = _gelu_gate(conv(ug_ref, 0), conv(uv_ref, D_FF, 0.5)).astype(BF16)
    o_ref[0] = x1[hr:, :] + jnp.dot(act, wd_ref[...], preferred_element_type=F32)


def _mlp(attn, y, x, w_out, n2, w_up, cw, cb, w_down):
    b, s, d = x.shape
    tm = FFN_ROWS
    hr = SUBLANES_BF16
    row_blk = lambda w: pl.BlockSpec((1, tm, w), lambda bi, i: (bi, i, 0))
    hist_blk = lambda w: pl.BlockSpec(
        (1, hr, w), lambda bi, i: (bi, jnp.maximum(i * (tm // hr) - 1, 0), 0))
    ratio = ATTN_BLOCK // tm
    nq = s // ATTN_BLOCK
    nh = nq // 2

    def fold(i):
        qb = i // ratio
        return qb // nh, jnp.where(qb < nh, qb, nq - 1 - qb) * ratio + i % ratio

    attn_blk = pl.BlockSpec((1, 1, tm, ATTN_WIDTH), lambda bi, i: (bi, *fold(i), 0))

    def attn_hist_map(bi, i):
        half, blk = fold(jnp.maximum(i - 1, 0))
        return (bi, half, blk * (tm // hr) + (tm // hr - 1), 0)

    attn_hist_blk = pl.BlockSpec((1, 1, hr, ATTN_WIDTH), attn_hist_map)
    resident = lambda shape: pl.BlockSpec(shape, lambda bi, i: (0,) * len(shape),
                                          pipeline_mode=pl.Buffered(1))
    return pl.pallas_call(
        _mlp_kernel,
        grid=(b, s // tm),
        in_specs=[attn_blk, attn_hist_blk, row_blk(RNN_WIDTH),
                  hist_blk(RNN_WIDTH), row_blk(d), hist_blk(d),
                  resident((d, d)), resident((1, d)), resident((d, 2 * D_FF)),
                  resident((FFN_CONV_WIDTH, 2 * D_FF)), resident((1, 2 * D_FF)),
                  resident((D_FF, d))],
        out_specs=row_blk(d),
        out_shape=jax.ShapeDtypeStruct((b, s, d), F32),
        scratch_shapes=[pltpu.VMEM((hr + tm, D_FF), F32)] * 2,
        compiler_params=pltpu.CompilerParams(
            dimension_semantics=("arbitrary", "arbitrary"),
            vmem_limit_bytes=VMEM_LIMIT_BYTES),
        name="mlp",
    )(attn, attn, y, y, x, x, w_out, n2, w_up, cw, cb, w_down)


def _rope_tables(seq):
    f32 = np.float32
    pos = np.arange(seq, dtype=f32)
    inv_freq = f32(ROPE_THETA) ** (-(np.arange(0, ROT_DIM, 2, dtype=f32) / f32(ROT_DIM)))
    ang = pos[:, None] * inv_freq[None, :].astype(f32)
    cos, sin = np.cos(ang).astype(f32), np.sin(ang).astype(f32)
    half = ROT_DIM // 2
    pad = QK_HEAD_DIM - ROT_DIM
    one = np.ones((seq, pad), f32)
    zero_h = np.zeros((seq, half), f32)
    zero_p = np.zeros((seq, pad), f32)
    rc = np.concatenate([cos, cos, one], axis=1)
    rs1 = np.concatenate([-sin, zero_h, zero_p], axis=1)
    rs2 = np.concatenate([zero_h, sin, zero_p], axis=1)
    tables = (cos.T, sin.T) + tuple(np.tile(t, (1, 2)) for t in (rc, rs1, rs2))
    return tuple(jnp.asarray(np.ascontiguousarray(t)) for t in tables)


def _block_diag(w):
    h, i, j = w.shape
    eye = jnp.eye(h, dtype=w.dtype)
    return (eye[:, None, :, None] * w[:, :, None, :]).reshape(h * i, h * j)


def kernel(x, norm1_w, w_in, q_norm_w, k_norm_w, lambda_q1, lambda_k1, lambda_q2, lambda_k2,
           subln_w, conv_rnn_w, conv_rnn_b, w_gate_a, b_gate_a, w_gate_x, b_gate_x, lru_lambda,
           rnn_norm_w, w_out, norm2_w, w_up, conv_ffn_w, conv_ffn_b, w_down):
    b, s, d = x.shape
    ct, st, rc, rs1, rs2 = _rope_tables(s)
    row = lambda v: v.reshape(1, -1)
    for l in range(DEPTH):
        lambda_init = 0.8 - 0.6 * math.exp(-0.3 * l)
        w = w_in[l].astype(BF16)
        v0 = 2 * QK_WIDTH
        wt = jnp.concatenate([w[:, :QK_WIDTH], w[:, v0:v0 + ATTN_WIDTH]], axis=1).T
        w_rest = jnp.concatenate([w[:, QK_WIDTH:v0], w[:, v0 + ATTN_WIDTH:]], axis=1)
        qt, k, vt, xr, gr, w_out16, w_up16, w_down16 = _in_proj(
            x, row(norm1_w[l]), wt, w_rest, q_norm_w[l].reshape(-1, 1),
            row(jnp.tile(k_norm_w[l], 2)), ct, st, rc, rs1, rs2,
            (w_out[l], w_up[l], w_down[l]))
        attn = _attention(qt, k, vt, row(lambda_q1[l]), row(lambda_k1[l]), row(lambda_q2[l]),
                          row(lambda_k2[l]), row(subln_w[l]), lambda_init)
        y = _rglru(xr, gr, conv_rnn_w[l], row(conv_rnn_b[l]),
                   _block_diag(w_gate_a[l]).astype(BF16), row(b_gate_a[l]),
                   _block_diag(w_gate_x[l]).astype(BF16), row(b_gate_x[l]),
                   row(lru_lambda[l]), row(rnn_norm_w[l]))
        x = _mlp(attn, y, x, w_out16, row(norm2_w[l]), w_up16,
                 conv_ffn_w[l], row(conv_ffn_b[l]), w_down16)
    return x
```

```python
import functools
import math

import jax
import jax.numpy as jnp
import numpy as np
from jax import lax
from jax.experimental import pallas as pl
from jax.experimental.pallas import tpu as pltpu

D_MODEL = 1024
DEPTH = 1
ATTN_HEADS = 4
QK_HEAD_DIM = 64
V_HEAD_DIM = 2 * QK_HEAD_DIM
ATTN_WIDTH = ATTN_HEADS * V_HEAD_DIM
QK_WIDTH = ATTN_HEADS * 2 * QK_HEAD_DIM
ROT_DIM = QK_HEAD_DIM // 4
ROPE_THETA = 500000.0
RNN_WIDTH = D_MODEL - ATTN_WIDTH
RNN_HEADS = 8
RNN_BLOCK = RNN_WIDTH // RNN_HEADS
RNN_CONV_WIDTH = 4
LRU_C = 8.0
D_FF = 3 * D_MODEL
FFN_CONV_WIDTH = 3
EPS = 1e-6

LANES = 128
SUBLANES_F32 = 8
SUBLANES_BF16 = 16
VMEM_LIMIT_BYTES = 56 * 1024 * 1024

PROJ_ROWS = 512
ATTN_BLOCK = 512
ATTN_GROUP = 4
RNN_ROWS = 512
FFN_ROWS = 512

LOG2E = math.log2(math.e)
F32 = jnp.float32
BF16 = jnp.bfloat16


def _gelu_gate(x, half_v):
    c = math.sqrt(2.0 / math.pi)
    t = jnp.tanh(x * (c + (c * 0.044715) * (x * x)))
    h = x * half_v
    return h + h * t


def _in_proj_kernel(x_ref, n1_ref, wt_ref, w_ref, qw_ref, kw_ref, ct_ref, st_ref, rc_ref,
                    rs1_ref, rs2_ref, wo_ref, wu_ref, wd_ref,
                    qt_ref, k_ref, vt_ref, xr_ref, gr_ref, wo16_ref, wu16_ref, wd16_ref):
    wo16_ref[...] = wo_ref[...].astype(BF16)
    wu16_ref[...] = wu_ref[...].astype(BF16)
    wd16_ref[...] = wd_ref[...].astype(BF16)

    x = x_ref[0]
    ms = jnp.mean(x * x, axis=-1, keepdims=True)
    h = (x * lax.rsqrt(ms + EPS) * n1_ref[...]).astype(BF16)

    half = ROT_DIM // 2
    lane = lax.broadcasted_iota(jnp.int32, (1, LANES), 1)
    lo = lane < QK_HEAD_DIM
    rc, rs1, rs2 = rc_ref[...], rs1_ref[...], rs2_ref[...]
    pk = jnp.dot(h, w_ref[:, 0:QK_WIDTH], preferred_element_type=F32)
    kw = kw_ref[...]
    for hd in range(ATTN_HEADS):
        y = pk[:, LANES * hd:LANES * (hd + 1)]
        sq = y * y
        s_all = jnp.sum(sq, axis=-1, keepdims=True)
        s_lo = jnp.sum(jnp.where(lo, sq, 0.0), axis=-1, keepdims=True)
        msq = jnp.where(lo, s_lo, s_all - s_lo) * (1.0 / QK_HEAD_DIM)
        y = y * lax.rsqrt(msq + EPS) * kw
        up = pltpu.roll(y, LANES - half, axis=1)
        dn = pltpu.roll(y, half, axis=1)
        k_ref[0, :, LANES * hd:LANES * (hd + 1)] = (y * rc + up * rs1 + dn * rs2).astype(BF16)
    c = QK_WIDTH
    xr_ref[0] = jnp.dot(h, w_ref[:, c:c + RNN_WIDTH], preferred_element_type=F32)
    c += RNN_WIDTH
    gr_ref[0] = jnp.dot(h, w_ref[:, c:c + RNN_WIDTH], preferred_element_type=F32)

    nt = (((1,), (1,)), ((), ()))
    pq = lax.dot_general(wt_ref[0:QK_WIDTH, :], h, nt, preferred_element_type=F32)
    ct, st = ct_ref[...], st_ref[...]
    q_scale = LOG2E * QK_HEAD_DIM ** -0.5
    for blk in range(QK_WIDTH // QK_HEAD_DIM):
        r0 = blk * QK_HEAD_DIM
        z = pq[r0:r0 + QK_HEAD_DIM, :]
        msq = jnp.mean(z * z, axis=0, keepdims=True)
        z = z * lax.rsqrt(msq + EPS) * qw_ref[...]
        x1, x2 = z[0:half, :], z[half:ROT_DIM, :]
        z = jnp.concatenate([x1 * ct - x2 * st, x2 * ct + x1 * st, z[ROT_DIM:, :]], axis=0)
        qt_ref[0, r0:r0 + QK_HEAD_DIM, :] = (z * q_scale).astype(BF16)
    vt_ref[0] = lax.dot_general(wt_ref[QK_WIDTH:, :], h, nt,
                                preferred_element_type=F32).astype(BF16)


def _in_proj(x, n1, wt, w_rest, qw_col, kw2, ct, st, rc, rs1, rs2, later_weights):
    b, s, d = x.shape
    tm = PROJ_ROWS
    n_steps = b * (s // tm)

    def slab(w):
        rows, cols = w.shape
        assert rows % (n_steps * SUBLANES_BF16) == 0
        return pl.BlockSpec((rows // n_steps, cols), lambda bi, i: (bi * (s // tm) + i, 0))

    row_blk = lambda w: pl.BlockSpec((1, tm, w), lambda bi, i: (bi, i, 0))
    col_blk = lambda w: pl.BlockSpec((1, w, tm), lambda bi, i: (bi, 0, i))
    const = lambda shape: pl.BlockSpec(shape, lambda bi, i: (0,) * len(shape))
    rope_blk = pl.BlockSpec((tm, LANES), lambda bi, i: (i, 0))
    rope_t_blk = pl.BlockSpec((ROT_DIM // 2, tm), lambda bi, i: (0, i))
    return pl.pallas_call(
        _in_proj_kernel,
        grid=(b, s // tm),
        in_specs=[row_blk(d), const((1, d)), const(wt.shape), const(w_rest.shape),
                  const((QK_HEAD_DIM, 1)), const((1, LANES)), rope_t_blk, rope_t_blk,
                  rope_blk, rope_blk, rope_blk] + [slab(w) for w in later_weights],
        out_specs=[col_blk(QK_WIDTH), row_blk(QK_WIDTH), col_blk(ATTN_WIDTH),
                   row_blk(RNN_WIDTH), row_blk(RNN_WIDTH)] + [slab(w) for w in later_weights],
        out_shape=[jax.ShapeDtypeStruct((b, QK_WIDTH, s), BF16),
                   jax.ShapeDtypeStruct((b, s, QK_WIDTH), BF16),
                   jax.ShapeDtypeStruct((b, ATTN_WIDTH, s), BF16),
                   jax.ShapeDtypeStruct((b, s, RNN_WIDTH), F32),
                   jax.ShapeDtypeStruct((b, s, RNN_WIDTH), F32)]
        + [jax.ShapeDtypeStruct(w.shape, BF16) for w in later_weights],
        compiler_params=pltpu.CompilerParams(
            dimension_semantics=("arbitrary", "arbitrary"),
            vmem_limit_bytes=VMEM_LIMIT_BYTES),
        name="in_proj",
    )(x, n1, wt, w_rest, qw_col, kw2, ct, st, rc, rs1, rs2, *later_weights)


def _attn_kernel(qlo_ref, qhi_ref, k_ref, vt_ref, lq1_ref, lk1_ref, lq2_ref, lk2_ref, sw_ref,
                 o_ref, qz_ref, m_ref, acc_ref, sa_ref, mxa_ref, sb_ref,
                 mxb_ref, *, lambda_init, n_blocks):
    t = ATTN_BLOCK
    lo = pl.program_id(2)
    hi = n_blocks - 1 - lo
    dim = lax.broadcasted_iota(jnp.int32, (LANES, 1), 0)
    for w, q_ref in enumerate((qlo_ref, qhi_ref)):
        qt = q_ref[0]
        zero = jnp.zeros_like(qt)
        qz_ref[w, :, 0:t] = jnp.where(dim < QK_HEAD_DIM, qt, zero)
        qz_ref[w, :, t:2 * t] = jnp.where(dim >= QK_HEAD_DIM, qt, zero)

    m_ref[...] = jnp.full(m_ref.shape, -1e30, F32)
    acc_ref[...] = jnp.zeros(acc_ref.shape, F32)

    def scores(buf, w, j, masked):
        s_ref, mx_ref = buf
        r0 = pl.multiple_of(j * t, t)
        kb = k_ref[0, pl.ds(r0, t), :]
        s = jnp.dot(kb, qz_ref[w], preferred_element_type=F32)
        if masked:
            key = lax.broadcasted_iota(jnp.int32, (t, 2 * t), 0)
            qry = lax.broadcasted_iota(jnp.int32, (t, 2 * t), 1) & (t - 1)
            s = jnp.where(key <= qry, s, -jnp.inf)
        s_ref[...] = s
        mx_ref[...] = jnp.max(s, axis=0, keepdims=True)

    def accumulate(buf, w, j):
        s_ref, mx_ref = buf
        r0 = pl.multiple_of(j * t, t)
        vtb = jnp.concatenate([vt_ref[0, :, pl.ds(r0, t)],
                               jnp.ones((SUBLANES_BF16, t), BF16)], axis=0)
        m_prev = m_ref[w]
        m_new = jnp.maximum(m_prev, mx_ref[...])
        alpha = jnp.exp2(m_prev - m_new)
        e = jnp.exp2(s_ref[...] - m_new)
        acc_ref[w] = alpha * acc_ref[w] + jnp.dot(vtb, e.astype(BF16),
                                                  preferred_element_type=F32)
        m_ref[w] = m_new

    n_items = n_blocks + 1
    group = ATTN_GROUP
    assert (n_items - 1) % group == 0

    def item(n):
        if isinstance(n, int) and n < 2:
            return (n, (lo, hi)[n])
        is_hi = (n - 2) >= lo
        return (is_hi.astype(jnp.int32), jnp.where(is_hi, n - 2 - lo, n - 2))

    bufs = ((sa_ref, mxa_ref), (sb_ref, mxb_ref))

    def run_group(first, masked_until):
        for r in range(group):
            n = first + r
            scores(bufs[(r + 1) % 2], *item(n + 1),
                   isinstance(n, int) and n + 1 < masked_until)
            accumulate(bufs[r % 2], *item(n))

    scores(bufs[0], *item(0), True)
    run_group(0, 2)

    def trip(i, carry):
        run_group(i * group, 0)
        return carry

    lax.fori_loop(1, (n_items - 1) // group, trip, 0)
    accumulate(bufs[0], *item(n_items - 1))

    lam = (jnp.exp(jnp.sum(lq1_ref[...] * lk1_ref[...], axis=-1, keepdims=True))
           - jnp.exp(jnp.sum(lq2_ref[...] * lk2_ref[...], axis=-1, keepdims=True))
           + lambda_init)
    dv = V_HEAD_DIM
    for w in range(2):
        l0 = acc_ref[w, dv:dv + 1, 0:t]
        l1 = acc_ref[w, dv:dv + 1, t:2 * t]
        o = acc_ref[w, 0:dv, 0:t] * (1.0 / l0) - acc_ref[w, 0:dv, t:2 * t] * (lam / l1)
        ms = jnp.mean(o * o, axis=0, keepdims=True)
        o = o * (lax.rsqrt(ms + EPS) * (1.0 - lambda_init))
        o_ref[0, w] = (o.T * sw_ref[...]).astype(BF16)


def _attention(qt, k, vt, lq1, lk1, lq2, lk2, sw, lambda_init):
    b, s, _ = k.shape
    t = ATTN_BLOCK
    nb = s // t
    half = nb // 2
    vec = lambda n: pl.BlockSpec((1, n), lambda bi, h, p: (0, 0))
    return pl.pallas_call(
        functools.partial(_attn_kernel, lambda_init=lambda_init, n_blocks=nb),
        grid=(b, ATTN_HEADS, half),
        in_specs=[pl.BlockSpec((1, LANES, t), lambda bi, h, p: (bi, h, p)),
                  pl.BlockSpec((1, LANES, t), lambda bi, h, p: (bi, h, nb - 1 - p)),
                  pl.BlockSpec((1, s, LANES), lambda bi, h, p: (bi, 0, h)),
                  pl.BlockSpec((1, LANES, s), lambda bi, h, p: (bi, h, 0)),
                  vec(QK_HEAD_DIM), vec(QK_HEAD_DIM), vec(QK_HEAD_DIM), vec(QK_HEAD_DIM),
                  vec(V_HEAD_DIM)],
        out_specs=pl.BlockSpec((1, 2, t, LANES), lambda bi, h, p: (bi, 0, p, h)),
        out_shape=jax.ShapeDtypeStruct((b, 2, s // 2, ATTN_WIDTH), BF16),
        scratch_shapes=[pltpu.VMEM((2, LANES, 2 * t), BF16),
                        pltpu.VMEM((2, 1, 2 * t), F32),
                        pltpu.VMEM((2, V_HEAD_DIM + SUBLANES_BF16, 2 * t), F32),
                        pltpu.VMEM((t, 2 * t), F32), pltpu.VMEM((1, 2 * t), F32),
                        pltpu.VMEM((t, 2 * t), F32), pltpu.VMEM((1, 2 * t), F32)],
        compiler_params=pltpu.CompilerParams(
            dimension_semantics=("arbitrary", "arbitrary", "arbitrary"),
            vmem_limit_bytes=VMEM_LIMIT_BYTES),
        name="diff_attn",
    )(qt, qt, k, vt, lq1, lk1, lq2, lk2, sw)


def _rglru_kernel(xr_ref, gr_ref, cw_ref, cb_ref, wa_ref, ba_ref, wx_ref, bx_ref, lam_ref,
                  nw_ref, y_ref, ext_ref, a_ref, b_ref, hc_ref):
    ts = RNN_ROWS
    halo = SUBLANES_F32

    @pl.when(pl.program_id(1) == 0)
    def _():
        ext_ref[0:halo, :] = jnp.zeros((halo, RNN_WIDTH), F32)
        hc_ref[...] = jnp.zeros((halo, RNN_WIDTH), F32)

    x = xr_ref[0]
    ext_ref[halo:halo + ts, :] = x
    xc = cb_ref[...] + cw_ref[RNN_CONV_WIDTH - 1:RNN_CONV_WIDTH, :] * x
    for d in range(1, RNN_CONV_WIDTH):
        tap = RNN_CONV_WIDTH - 1 - d
        xc = xc + cw_ref[tap:tap + 1, :] * ext_ref[halo - d:halo - d + ts, :]
    ext_ref[0:halo, :] = x[ts - halo:ts, :]

    xb = xc.astype(BF16)
    r = jax.nn.sigmoid(jnp.dot(xb, wa_ref[...], preferred_element_type=F32) + ba_ref[...])
    ig = jax.nn.sigmoid(jnp.dot(xb, wx_ref[...], preferred_element_type=F32) + bx_ref[...])
    z = -lam_ref[...]
    softplus = jnp.maximum(z, 0.0) + jnp.log1p(jnp.exp(-jnp.abs(z)))
    log_a = r * (-LRU_C * softplus)
    a = jnp.exp(log_a)
    w = 1.0 - a * a
    mult = jnp.where(w > 0.0, w * lax.rsqrt(w), 0.0)
    u = mult * ig * xc

    tiles = (ts // SUBLANES_F32, SUBLANES_F32, RNN_WIDTH)
    a = a.reshape(tiles)
    u = u.reshape(tiles)
    sub = lax.broadcasted_iota(jnp.int32, (1, SUBLANES_F32, 1), 1)
    d = 1
    while d < SUBLANES_F32:
        keep = sub >= d
        a_prev = jnp.where(keep, pltpu.roll(a, d, axis=1), 1.0)
        u_prev = jnp.where(keep, pltpu.roll(u, d, axis=1), 0.0)
        u = a * u_prev + u
        a = a * a_prev
        d *= 2
    h_prev = hc_ref[...]
    hs = []
    for j in range(ts // SUBLANES_F32):
        h = u[j] + a[j] * h_prev
        hs.append(h)
        h_prev = jnp.broadcast_to(h[SUBLANES_F32 - 1:SUBLANES_F32, :], h.shape)
    hc_ref[...] = h_prev

    y = _gelu_gate(gr_ref[0], 0.5 * jnp.concatenate(hs, axis=0))
    ms = jnp.mean(y * y, axis=-1, keepdims=True)
    y_ref[0] = (y * lax.rsqrt(ms + EPS) * nw_ref[...]).astype(BF16)


def _rglru(xr, gr, cw, cb, wa, ba, wx, bx, lam, nw):
    b, s, c = xr.shape
    ts = RNN_ROWS
    row_blk = pl.BlockSpec((1, ts, c), lambda bi, i: (bi, i, 0))
    const = lambda shape: pl.BlockSpec(shape, lambda bi, i: (0,) * len(shape))
    return pl.pallas_call(
        _rglru_kernel,
        grid=(b, s // ts),
        in_specs=[row_blk, row_blk, const((RNN_CONV_WIDTH, c)), const((1, c)),
                  const((c, c)), const((1, c)), const((c, c)), const((1, c)),
                  const((1, c)), const((1, c))],
        out_specs=row_blk,
        out_shape=jax.ShapeDtypeStruct((b, s, c), BF16),
        scratch_shapes=[pltpu.VMEM((ts + SUBLANES_F32, c), F32),
                        pltpu.VMEM((ts, c), F32), pltpu.VMEM((ts, c), F32),
                        pltpu.VMEM((SUBLANES_F32, c), F32)],
        compiler_params=pltpu.CompilerParams(
            dimension_semantics=("arbitrary", "arbitrary"),
            vmem_limit_bytes=VMEM_LIMIT_BYTES),
        name="rglru",
    )(xr, gr, cw, cb, wa, ba, wx, bx, lam, nw)


def _mlp_kernel(attn_ref, y_ref, x_ref, wo_ref, n2_ref, wu_ref, cw_ref, cb_ref, wd_ref,
                o_ref, ug_ref, uv_ref):
    tm = FFN_ROWS
    hr = SUBLANES_F32

    @pl.when(pl.program_id(1) == 0)
    def _():
        ug_ref[0:hr, :] = jnp.zeros((hr, D_FF), F32)
        uv_ref[0:hr, :] = jnp.zeros((hr, D_FF), F32)

    x1 = x_ref[0] + jnp.dot(attn_ref[0, 0], wo_ref[0:ATTN_WIDTH, :],
                            preferred_element_type=F32)
    x1 = x1 + jnp.dot(y_ref[0], wo_ref[ATTN_WIDTH:, :], preferred_element_type=F32)
    ms = jnp.mean(x1 * x1, axis=-1, keepdims=True)
    h2 = (x1 * lax.rsqrt(ms + EPS) * n2_ref[...]).astype(BF16)

    def conv(u_ref, c0, scale=1.0):
        cw = cw_ref[:, c0:c0 + D_FF] * scale
        y = cb_ref[:, c0:c0 + D_FF] * scale + (
            cw[FFN_CONV_WIDTH - 1:FFN_CONV_WIDTH, :] * u_ref[hr:hr + tm, :])
        for d in range(1, FFN_CONV_WIDTH):
            tap = FFN_CONV_WIDTH - 1 - d
            y = y + cw[tap:tap + 1, :] * u_ref[hr - d:hr - d + tm, :]
        return y

    ug_ref[hr:hr + tm, :] = jnp.dot(h2, wu_ref[:, 0:D_FF], preferred_element_type=F32)
    uv_ref[hr:hr + tm, :] = jnp.dot(h2, wu_ref[:, D_FF:], preferred_element_type=F32)
    act = _gelu_gate(conv(ug_ref, 0), conv(uv_ref, D_FF, 0.5)).astype(BF16)
    ug_ref[0:hr, :] = ug_ref[tm:tm + hr, :]
    uv_ref[0:hr, :] = uv_ref[tm:tm + hr, :]
    o_ref[0] = x1 + jnp.dot(act, wd_ref[...], preferred_element_type=F32)


def _mlp(attn, y, x, w_out, n2, w_up, cw, cb, w_down):
    b, s, d = x.shape
    tm = FFN_ROWS
    hr = SUBLANES_F32
    row_blk = lambda w: pl.BlockSpec((1, tm, w), lambda bi, i: (bi, i, 0))
    ratio = ATTN_BLOCK // tm
    nq = s // ATTN_BLOCK
    nh = nq // 2

    def fold(i):
        qb = i // ratio
        return qb // nh, jnp.where(qb < nh, qb, nq - 1 - qb) * ratio + i % ratio

    attn_blk = pl.BlockSpec((1, 1, tm, ATTN_WIDTH), lambda bi, i: (bi, *fold(i), 0))
    resident = lambda shape: pl.BlockSpec(shape, lambda bi, i: (0,) * len(shape),
                                          pipeline_mode=pl.Buffered(1))
    return pl.pallas_call(
        _mlp_kernel,
        grid=(b, s // tm),
        in_specs=[attn_blk, row_blk(RNN_WIDTH), row_blk(d),
                  resident((d, d)), resident((1, d)), resident((d, 2 * D_FF)),
                  resident((FFN_CONV_WIDTH, 2 * D_FF)), resident((1, 2 * D_FF)),
                  resident((D_FF, d))],
        out_specs=row_blk(d),
        out_shape=jax.ShapeDtypeStruct((b, s, d), F32),
        scratch_shapes=[pltpu.VMEM((hr + tm, D_FF), F32)] * 2,
        compiler_params=pltpu.CompilerParams(
            dimension_semantics=("arbitrary", "arbitrary"),
            vmem_limit_bytes=VMEM_LIMIT_BYTES),
        name="mlp",
    )(attn, y, x, w_out, n2, w_up, cw, cb, w_down)


def _rope_tables(seq):
    f32 = np.float32
    pos = np.arange(seq, dtype=f32)
    inv_freq = f32(ROPE_THETA) ** (-(np.arange(0, ROT_DIM, 2, dtype=f32) / f32(ROT_DIM)))
    ang = pos[:, None] * inv_freq[None, :].astype(f32)
    cos, sin = np.cos(ang).astype(f32), np.sin(ang).astype(f32)
    half = ROT_DIM // 2
    pad = QK_HEAD_DIM - ROT_DIM
    one = np.ones((seq, pad), f32)
    zero_h = np.zeros((seq, half), f32)
    zero_p = np.zeros((seq, pad), f32)
    rc = np.concatenate([cos, cos, one], axis=1)
    rs1 = np.concatenate([-sin, zero_h, zero_p], axis=1)
    rs2 = np.concatenate([zero_h, sin, zero_p], axis=1)
    tables = (cos.T, sin.T) + tuple(np.tile(t, (1, 2)) for t in (rc, rs1, rs2))
    return tuple(jnp.asarray(np.ascontiguousarray(t)) for t in tables)


def _block_diag(w):
    h, i, j = w.shape
    eye = jnp.eye(h, dtype=w.dtype)
    return (eye[:, None, :, None] * w[:, :, None, :]).reshape(h * i, h * j)


def kernel(x, norm1_w, w_in, q_norm_w, k_norm_w, lambda_q1, lambda_k1, lambda_q2, lambda_k2,
           subln_w, conv_rnn_w, conv_rnn_b, w_gate_a, b_gate_a, w_gate_x, b_gate_x, lru_lambda,
           rnn_norm_w, w_out, norm2_w, w_up, conv_ffn_w, conv_ffn_b, w_down):
    b, s, d = x.shape
    ct, st, rc, rs1, rs2 = _rope_tables(s)
    row = lambda v: v.reshape(1, -1)
    for l in range(DEPTH):
        lambda_init = 0.8 - 0.6 * math.exp(-0.3 * l)
        w = w_in[l].astype(BF16)
        v0 = 2 * QK_WIDTH
        wt = jnp.concatenate([w[:, :QK_WIDTH], w[:, v0:v0 + ATTN_WIDTH]], axis=1).T
        w_rest = jnp.concatenate([w[:, QK_WIDTH:v0], w[:, v0 + ATTN_WIDTH:]], axis=1)
        qt, k, vt, xr, gr, w_out16, w_up16, w_down16 = _in_proj(
            x, row(norm1_w[l]), wt, w_rest, q_norm_w[l].reshape(-1, 1),
            row(jnp.tile(k_norm_w[l], 2)), ct, st, rc, rs1, rs2,
            (w_out[l], w_up[l], w_down[l]))
        attn = _attention(qt, k, vt, row(lambda_q1[l]), row(lambda_k1[l]), row(lambda_q2[l]),
                          row(lambda_k2[l]), row(subln_w[l]), lambda_init)
        y = _rglru(xr, gr, conv_rnn_w[l], row(conv_rnn_b[l]),
                   _block_diag(w_gate_a[l]).astype(BF16), row(b_gate_a[l]),
                   _block_diag(w_gate_x[l]).astype(BF16), row(b_gate_x[l]),
                   row(lru_lambda[l]), row(rnn_norm_w[l]))
        x = _mlp(attn, y, x, w_out16, row(norm2_w[l]), w_up16,
                 conv_ffn_w[l], row(conv_ffn_b[l]), w_down16)
    return x
```

```python
import functools
import math

import jax
import jax.numpy as jnp
import numpy as np
from jax import lax
from jax.experimental import pallas as pl
from jax.experimental.pallas import tpu as pltpu

D_MODEL = 1024
DEPTH = 1
ATTN_HEADS = 4
QK_HEAD_DIM = 64
V_HEAD_DIM = 2 * QK_HEAD_DIM
ATTN_WIDTH = ATTN_HEADS * V_HEAD_DIM
QK_WIDTH = ATTN_HEADS * 2 * QK_HEAD_DIM
ROT_DIM = QK_HEAD_DIM // 4
ROPE_THETA = 500000.0
RNN_WIDTH = D_MODEL - ATTN_WIDTH
RNN_HEADS = 8
RNN_BLOCK = RNN_WIDTH // RNN_HEADS
RNN_CONV_WIDTH = 4
LRU_C = 8.0
D_FF = 3 * D_MODEL
FFN_CONV_WIDTH = 3
EPS = 1e-6

LANES = 128
SUBLANES_F32 = 8
SUBLANES_BF16 = 16
VMEM_LIMIT_BYTES = 56 * 1024 * 1024

PROJ_ROWS = 1024
ATTN_BLOCK = 512
ATTN_GROUP = 4
RNN_ROWS = 512
FFN_ROWS = 512

LOG2E = math.log2(math.e)
F32 = jnp.float32
BF16 = jnp.bfloat16


def _gelu_gate(x, half_v):
    c = math.sqrt(2.0 / math.pi)
    t = jnp.tanh(x * (c + (c * 0.044715) * (x * x)))
    h = x * half_v
    return h + h * t


def _in_proj_kernel(x_ref, n1_ref, wt_ref, w_ref, qw_ref, kw_ref, ct_ref, st_ref, rc_ref,
                    rs1_ref, rs2_ref, wo_ref, wu_ref, wd_ref,
                    qt_ref, k_ref, vt_ref, xr_ref, gr_ref, wo16_ref, wu16_ref, wd16_ref):
    wo16_ref[...] = wo_ref[...].astype(BF16)
    wu16_ref[...] = wu_ref[...].astype(BF16)
    wd16_ref[...] = wd_ref[...].astype(BF16)

    x = x_ref[0]
    ms = jnp.mean(x * x, axis=-1, keepdims=True)
    h = (x * lax.rsqrt(ms + EPS) * n1_ref[...]).astype(BF16)

    half = ROT_DIM // 2
    lane = lax.broadcasted_iota(jnp.int32, (1, LANES), 1)
    lo = lane < QK_HEAD_DIM
    rc, rs1, rs2 = rc_ref[...], rs1_ref[...], rs2_ref[...]
    pk = jnp.dot(h, w_ref[:, 0:QK_WIDTH], preferred_element_type=F32)
    kw = kw_ref[...]
    for hd in range(ATTN_HEADS):
        y = pk[:, LANES * hd:LANES * (hd + 1)]
        sq = y * y
        s_all = jnp.sum(sq, axis=-1, keepdims=True)
        s_lo = jnp.sum(jnp.where(lo, sq, 0.0), axis=-1, keepdims=True)
        msq = jnp.where(lo, s_lo, s_all - s_lo) * (1.0 / QK_HEAD_DIM)
        y = y * lax.rsqrt(msq + EPS) * kw
        up = pltpu.roll(y, LANES - half, axis=1)
        dn = pltpu.roll(y, half, axis=1)
        k_ref[0, :, LANES * hd:LANES * (hd + 1)] = (y * rc + up * rs1 + dn * rs2).astype(BF16)
    c = QK_WIDTH
    xr_ref[0] = jnp.dot(h, w_ref[:, c:c + RNN_WIDTH], preferred_element_type=F32)
    c += RNN_WIDTH
    gr_ref[0] = jnp.dot(h, w_ref[:, c:c + RNN_WIDTH], preferred_element_type=F32)

    nt = (((1,), (1,)), ((), ()))
    pq = lax.dot_general(wt_ref[0:QK_WIDTH, :], h, nt, preferred_element_type=F32)
    ct, st = ct_ref[...], st_ref[...]
    q_scale = LOG2E * QK_HEAD_DIM ** -0.5
    for blk in range(QK_WIDTH // QK_HEAD_DIM):
        r0 = blk * QK_HEAD_DIM
        z = pq[r0:r0 + QK_HEAD_DIM, :]
        msq = jnp.mean(z * z, axis=0, keepdims=True)
        z = z * lax.rsqrt(msq + EPS) * qw_ref[...]
        x1, x2 = z[0:half, :], z[half:ROT_DIM, :]
        z = jnp.concatenate([x1 * ct - x2 * st, x2 * ct + x1 * st, z[ROT_DIM:, :]], axis=0)
        qt_ref[0, r0:r0 + QK_HEAD_DIM, :] = (z * q_scale).astype(BF16)
    vt_ref[0] = lax.dot_general(wt_ref[QK_WIDTH:, :], h, nt,
                                preferred_element_type=F32).astype(BF16)


def _in_proj(x, n1, wt, w_rest, qw_col, kw2, ct, st, rc, rs1, rs2, later_weights):
    b, s, d = x.shape
    tm = PROJ_ROWS
    n_steps = b * (s // tm)

    def slab(w):
        rows, cols = w.shape
        assert rows % (n_steps * SUBLANES_BF16) == 0
        return pl.BlockSpec((rows // n_steps, cols), lambda bi, i: (bi * (s // tm) + i, 0))

    row_blk = lambda w: pl.BlockSpec((1, tm, w), lambda bi, i: (bi, i, 0))
    col_blk = lambda w: pl.BlockSpec((1, w, tm), lambda bi, i: (bi, 0, i))
    const = lambda shape: pl.BlockSpec(shape, lambda bi, i: (0,) * len(shape))
    rope_blk = pl.BlockSpec((tm, LANES), lambda bi, i: (i, 0))
    rope_t_blk = pl.BlockSpec((ROT_DIM // 2, tm), lambda bi, i: (0, i))
    return pl.pallas_call(
        _in_proj_kernel,
        grid=(b, s // tm),
        in_specs=[row_blk(d), const((1, d)), const(wt.shape), const(w_rest.shape),
                  const((QK_HEAD_DIM, 1)), const((1, LANES)), rope_t_blk, rope_t_blk,
                  rope_blk, rope_blk, rope_blk] + [slab(w) for w in later_weights],
        out_specs=[col_blk(QK_WIDTH), row_blk(QK_WIDTH), col_blk(ATTN_WIDTH),
                   row_blk(RNN_WIDTH), row_blk(RNN_WIDTH)] + [slab(w) for w in later_weights],
        out_shape=[jax.ShapeDtypeStruct((b, QK_WIDTH, s), BF16),
                   jax.ShapeDtypeStruct((b, s, QK_WIDTH), BF16),
                   jax.ShapeDtypeStruct((b, ATTN_WIDTH, s), BF16),
                   jax.ShapeDtypeStruct((b, s, RNN_WIDTH), F32),
                   jax.ShapeDtypeStruct((b, s, RNN_WIDTH), F32)]
        + [jax.ShapeDtypeStruct(w.shape, BF16) for w in later_weights],
        compiler_params=pltpu.CompilerParams(
            dimension_semantics=("arbitrary", "arbitrary"),
            vmem_limit_bytes=VMEM_LIMIT_BYTES),
        name="in_proj",
    )(x, n1, wt, w_rest, qw_col, kw2, ct, st, rc, rs1, rs2, *later_weights)


def _attn_kernel(qlo_ref, qhi_ref, k_ref, vt_ref, lq1_ref, lk1_ref, lq2_ref, lk2_ref, sw_ref,
                 o_ref, qz_ref, m_ref, acc_ref, sa_ref, mxa_ref, sb_ref,
                 mxb_ref, *, lambda_init, n_blocks):
    t = ATTN_BLOCK
    lo = pl.program_id(2)
    hi = n_blocks - 1 - lo
    dim = lax.broadcasted_iota(jnp.int32, (LANES, 1), 0)
    for w, q_ref in enumerate((qlo_ref, qhi_ref)):
        qt = q_ref[0]
        zero = jnp.zeros_like(qt)
        qz_ref[w, :, 0:t] = jnp.where(dim < QK_HEAD_DIM, qt, zero)
        qz_ref[w, :, t:2 * t] = jnp.where(dim >= QK_HEAD_DIM, qt, zero)

    m_ref[...] = jnp.full(m_ref.shape, -1e30, F32)
    acc_ref[...] = jnp.zeros(acc_ref.shape, F32)

    def scores(buf, w, j, masked):
        s_ref, mx_ref = buf
        r0 = pl.multiple_of(j * t, t)
        kb = k_ref[0, pl.ds(r0, t), :]
        s = jnp.dot(kb, qz_ref[w], preferred_element_type=F32)
        if masked:
            key = lax.broadcasted_iota(jnp.int32, (t, 2 * t), 0)
            qry = lax.broadcasted_iota(jnp.int32, (t, 2 * t), 1) & (t - 1)
            s = jnp.where(key <= qry, s, -jnp.inf)
        s_ref[...] = s
        mx_ref[...] = jnp.max(s, axis=0, keepdims=True)

    def accumulate(buf, w, j):
        s_ref, mx_ref = buf
        r0 = pl.multiple_of(j * t, t)
        vtb = jnp.concatenate([vt_ref[0, :, pl.ds(r0, t)],
                               jnp.ones((SUBLANES_BF16, t), BF16)], axis=0)
        m_prev = m_ref[w]
        m_new = jnp.maximum(m_prev, mx_ref[...])
        alpha = jnp.exp2(m_prev - m_new)
        e = jnp.exp2(s_ref[...] - m_new)
        acc_ref[w] = alpha * acc_ref[w] + jnp.dot(vtb, e.astype(BF16),
                                                  preferred_element_type=F32)
        m_ref[w] = m_new

    n_items = n_blocks + 1
    group = ATTN_GROUP
    assert (n_items - 1) % group == 0

    def item(n):
        if isinstance(n, int) and n < 2:
            return (n, (lo, hi)[n])
        is_hi = (n - 2) >= lo
        return (is_hi.astype(jnp.int32), jnp.where(is_hi, n - 2 - lo, n - 2))

    bufs = ((sa_ref, mxa_ref), (sb_ref, mxb_ref))

    def run_group(first, masked_until):
        for r in range(group):
            n = first + r
            scores(bufs[(r + 1) % 2], *item(n + 1),
                   isinstance(n, int) and n + 1 < masked_until)
            accumulate(bufs[r % 2], *item(n))

    scores(bufs[0], *item(0), True)
    run_group(0, 2)

    def trip(i, carry):
        run_group(i * group, 0)
        return carry

    lax.fori_loop(1, (n_items - 1) // group, trip, 0)
    accumulate(bufs[0], *item(n_items - 1))

    lam = (jnp.exp(jnp.sum(lq1_ref[...] * lk1_ref[...], axis=-1, keepdims=True))
           - jnp.exp(jnp.sum(lq2_ref[...] * lk2_ref[...], axis=-1, keepdims=True))
           + lambda_init)
    dv = V_HEAD_DIM
    for w in range(2):
        l0 = acc_ref[w, dv:dv + 1, 0:t]
        l1 = acc_ref[w, dv:dv + 1, t:2 * t]
        o = acc_ref[w, 0:dv, 0:t] * (1.0 / l0) - acc_ref[w, 0:dv, t:2 * t] * (lam / l1)
        ms = jnp.mean(o * o, axis=0, keepdims=True)
        o = o * (lax.rsqrt(ms + EPS) * (1.0 - lambda_init))
        o_ref[0, w] = (o.T * sw_ref[...]).astype(BF16)


def _attention(qt, k, vt, lq1, lk1, lq2, lk2, sw, lambda_init):
    b, s, _ = k.shape
    t = ATTN_BLOCK
    nb = s // t
    half = nb // 2
    vec = lambda n: pl.BlockSpec((1, n), lambda bi, h, p: (0, 0))
    return pl.pallas_call(
        functools.partial(_attn_kernel, lambda_init=lambda_init, n_blocks=nb),
        grid=(b, ATTN_HEADS, half),
        in_specs=[pl.BlockSpec((1, LANES, t), lambda bi, h, p: (bi, h, p)),
                  pl.BlockSpec((1, LANES, t), lambda bi, h, p: (bi, h, nb - 1 - p)),
                  pl.BlockSpec((1, s, LANES), lambda bi, h, p: (bi, 0, h)),
                  pl.BlockSpec((1, LANES, s), lambda bi, h, p: (bi, h, 0)),
                  vec(QK_HEAD_DIM), vec(QK_HEAD_DIM), vec(QK_HEAD_DIM), vec(QK_HEAD_DIM),
                  vec(V_HEAD_DIM)],
        out_specs=pl.BlockSpec((1, 2, t, LANES), lambda bi, h, p: (bi, 0, p, h)),
        out_shape=jax.ShapeDtypeStruct((b, 2, s // 2, ATTN_WIDTH), BF16),
        scratch_shapes=[pltpu.VMEM((2, LANES, 2 * t), BF16),
                        pltpu.VMEM((2, 1, 2 * t), F32),
                        pltpu.VMEM((2, V_HEAD_DIM + SUBLANES_BF16, 2 * t), F32),
                        pltpu.VMEM((t, 2 * t), F32), pltpu.VMEM((1, 2 * t), F32),
                        pltpu.VMEM((t, 2 * t), F32), pltpu.VMEM((1, 2 * t), F32)],
        compiler_params=pltpu.CompilerParams(
            dimension_semantics=("arbitrary", "arbitrary", "arbitrary"),
            vmem_limit_bytes=VMEM_LIMIT_BYTES),
        name="diff_attn",
    )(qt, qt, k, vt, lq1, lk1, lq2, lk2, sw)


def _rglru_kernel(xr_ref, gr_ref, cw_ref, cb_ref, wa_ref, ba_ref, wx_ref, bx_ref, lam_ref,
                  nw_ref, y_ref, ext_ref, a_ref, b_ref, hc_ref):
    ts = RNN_ROWS
    halo = SUBLANES_F32

    @pl.when(pl.program_id(1) == 0)
    def _():
        ext_ref[0:halo, :] = jnp.zeros((halo, RNN_WIDTH), F32)
        hc_ref[...] = jnp.zeros((halo, RNN_WIDTH), F32)

    x = xr_ref[0]
    ext_ref[halo:halo + ts, :] = x
    xc = cb_ref[...] + cw_ref[RNN_CONV_WIDTH - 1:RNN_CONV_WIDTH, :] * x
    for d in range(1, RNN_CONV_WIDTH):
        tap = RNN_CONV_WIDTH - 1 - d
        xc = xc + cw_ref[tap:tap + 1, :] * ext_ref[halo - d:halo - d + ts, :]
    ext_ref[0:halo, :] = x[ts - halo:ts, :]

    xb = xc.astype(BF16)
    r = jax.nn.sigmoid(jnp.dot(xb, wa_ref[...], preferred_element_type=F32) + ba_ref[...])
    ig = jax.nn.sigmoid(jnp.dot(xb, wx_ref[...], preferred_element_type=F32) + bx_ref[...])
    z = -lam_ref[...]
    softplus = jnp.maximum(z, 0.0) + jnp.log1p(jnp.exp(-jnp.abs(z)))
    log_a = r * (-LRU_C * softplus)
    a = jnp.exp(log_a)
    w = 1.0 - a * a
    mult = jnp.where(w > 0.0, w * lax.rsqrt(w), 0.0)
    u = mult * ig * xc

    tiles = (ts // SUBLANES_F32, SUBLANES_F32, RNN_WIDTH)
    a = a.reshape(tiles)
    u = u.reshape(tiles)
    sub = lax.broadcasted_iota(jnp.int32, (1, SUBLANES_F32, 1), 1)
    d = 1
    while d < SUBLANES_F32:
        keep = sub >= d
        a_prev = jnp.where(keep, pltpu.roll(a, d, axis=1), 1.0)
        u_prev = jnp.where(keep, pltpu.roll(u, d, axis=1), 0.0)
        u = a * u_prev + u
        a = a * a_prev
        d *= 2
    h_prev = hc_ref[...]
    hs = []
    for j in range(ts // SUBLANES_F32):
        h = u[j] + a[j] * h_prev
        hs.append(h)
        h_prev = jnp.broadcast_to(h[SUBLANES_F32 - 1:SUBLANES_F32, :], h.shape)
    hc_ref[...] = h_prev

    y = _gelu_gate(gr_ref[0], 0.5 * jnp.concatenate(hs, axis=0))
    ms = jnp.mean(y * y, axis=-1, keepdims=True)
    y_ref[0] = (y * lax.rsqrt(ms + EPS) * nw_ref[...]).astype(BF16)


def _rglru(xr, gr, cw, cb, wa, ba, wx, bx, lam, nw):
    b, s, c = xr.shape
    ts = RNN_ROWS
    row_blk = pl.BlockSpec((1, ts, c), lambda bi, i: (bi, i, 0))
    const = lambda shape: pl.BlockSpec(shape, lambda bi, i: (0,) * len(shape))
    return pl.pallas_call(
        _rglru_kernel,
        grid=(b, s // ts),
        in_specs=[row_blk, row_blk, const((RNN_CONV_WIDTH, c)), const((1, c)),
                  const((c, c)), const((1, c)), const((c, c)), const((1, c)),
                  const((1, c)), const((1, c))],
        out_specs=row_blk,
        out_shape=jax.ShapeDtypeStruct((b, s, c), BF16),
        scratch_shapes=[pltpu.VMEM((ts + SUBLANES_F32, c), F32),
                        pltpu.VMEM((ts, c), F32), pltpu.VMEM((ts, c), F32),
                        pltpu.VMEM((SUBLANES_F32, c), F32)],
        compiler_params=pltpu.CompilerParams(
            dimension_semantics=("arbitrary", "arbitrary"),
            vmem_limit_bytes=VMEM_LIMIT_BYTES),
        name="rglru",
    )(xr, gr, cw, cb, wa, ba, wx, bx, lam, nw)


def _mlp_kernel(attn_ref, y_ref, x_ref, wo_ref, n2_ref, wu_ref, cw_ref, cb_ref, wd_ref,
                o_ref, ug_ref, uv_ref):
    tm = FFN_ROWS
    hr = SUBLANES_F32

    @pl.when(pl.program_id(1) == 0)
    def _():
        ug_ref[0:hr, :] = jnp.zeros((hr, D_FF), F32)
        uv_ref[0:hr, :] = jnp.zeros((hr, D_FF), F32)

    x1 = x_ref[0] + jnp.dot(attn_ref[0, 0], wo_ref[0:ATTN_WIDTH, :],
                            preferred_element_type=F32)
    x1 = x1 + jnp.dot(y_ref[0], wo_ref[ATTN_WIDTH:, :], preferred_element_type=F32)
    ms = jnp.mean(x1 * x1, axis=-1, keepdims=True)
    h2 = (x1 * lax.rsqrt(ms + EPS) * n2_ref[...]).astype(BF16)

    def conv(u_ref, c0, scale=1.0):
        cw = cw_ref[:, c0:c0 + D_FF] * scale
        y = cb_ref[:, c0:c0 + D_FF] * scale + (
            cw[FFN_CONV_WIDTH - 1:FFN_CONV_WIDTH, :] * u_ref[hr:hr + tm, :])
        for d in range(1, FFN_CONV_WIDTH):
            tap = FFN_CONV_WIDTH - 1 - d
            y = y + cw[tap:tap + 1, :] * u_ref[hr - d:hr - d + tm, :]
        return y

    ug_ref[hr:hr + tm, :] = jnp.dot(h2, wu_ref[:, 0:D_FF], preferred_element_type=F32)
    uv_ref[hr:hr + tm, :] = jnp.dot(h2, wu_ref[:, D_FF:], preferred_element_type=F32)
    act = _gelu_gate(conv(ug_ref, 0), conv(uv_ref, D_FF, 0.5)).astype(BF16)
    ug_ref[0:hr, :] = ug_ref[tm:tm + hr, :]
    uv_ref[0:hr, :] = uv_ref[tm:tm + hr, :]
    o_ref[0] = x1 + jnp.dot(act, wd_ref[...], preferred_element_type=F32)


def _mlp(attn, y, x, w_out, n2, w_up, cw, cb, w_down):
    b, s, d = x.shape
    tm = FFN_ROWS
    hr = SUBLANES_F32
    row_blk = lambda w: pl.BlockSpec((1, tm, w), lambda bi, i: (bi, i, 0))
    ratio = ATTN_BLOCK // tm
    nq = s // ATTN_BLOCK
    nh = nq // 2

    def fold(i):
        qb = i // ratio
        return qb // nh, jnp.where(qb < nh, qb, nq - 1 - qb) * ratio + i % ratio

    attn_blk = pl.BlockSpec((1, 1, tm, ATTN_WIDTH), lambda bi, i: (bi, *fold(i), 0))
    resident = lambda shape: pl.BlockSpec(shape, lambda bi, i: (0,) * len(shape),
                                          pipeline_mode=pl.Buffered(1))
    return pl.pallas_call(
        _mlp_kernel,
        grid=(b, s // tm),
        in_specs=[attn_blk, row_blk(RNN_WIDTH), row_blk(d),
                  resident((d, d)), resident((1, d)), resident((d, 2 * D_FF)),
                  resident((FFN_CONV_WIDTH, 2 * D_FF)), resident((1, 2 * D_FF)),
                  resident((D_FF, d))],
        out_specs=row_blk(d),
        out_shape=jax.ShapeDtypeStruct((b, s, d), F32),
        scratch_shapes=[pltpu.VMEM((hr + tm, D_FF), F32)] * 2,
        compiler_params=pltpu.CompilerParams(
            dimension_semantics=("arbitrary", "arbitrary"),
            vmem_limit_bytes=VMEM_LIMIT_BYTES),
        name="mlp",
    )(attn, y, x, w_out, n2, w_up, cw, cb, w_down)


def _rope_tables(seq):
    f32 = np.float32
    pos = np.arange(seq, dtype=f32)
    inv_freq = f32(ROPE_THETA) ** (-(np.arange(0, ROT_DIM, 2, dtype=f32) / f32(ROT_DIM)))
    ang = pos[:, None] * inv_freq[None, :].astype(f32)
    cos, sin = np.cos(ang).astype(f32), np.sin(ang).astype(f32)
    half = ROT_DIM // 2
    pad = QK_HEAD_DIM - ROT_DIM
    one = np.ones((seq, pad), f32)
    zero_h = np.zeros((seq, half), f32)
    zero_p = np.zeros((seq, pad), f32)
    rc = np.concatenate([cos, cos, one], axis=1)
    rs1 = np.concatenate([-sin, zero_h, zero_p], axis=1)
    rs2 = np.concatenate([zero_h, sin, zero_p], axis=1)
    tables = (cos.T, sin.T) + tuple(np.tile(t, (1, 2)) for t in (rc, rs1, rs2))
    return tuple(jnp.asarray(np.ascontiguousarray(t)) for t in tables)


def _block_diag(w):
    h, i, j = w.shape
    eye = jnp.eye(h, dtype=w.dtype)
    return (eye[:, None, :, None] * w[:, :, None, :]).reshape(h * i, h * j)


def kernel(x, norm1_w, w_in, q_norm_w, k_norm_w, lambda_q1, lambda_k1, lambda_q2, lambda_k2,
           subln_w, conv_rnn_w, conv_rnn_b, w_gate_a, b_gate_a, w_gate_x, b_gate_x, lru_lambda,
           rnn_norm_w, w_out, norm2_w, w_up, conv_ffn_w, conv_ffn_b, w_down):
    b, s, d = x.shape
    ct, st, rc, rs1, rs2 = _rope_tables(s)
    row = lambda v: v.reshape(1, -1)
    for l in range(DEPTH):
        lambda_init = 0.8 - 0.6 * math.exp(-0.3 * l)
        w = w_in[l].astype(BF16)
        v0 = 2 * QK_WIDTH
        wt = jnp.concatenate([w[:, :QK_WIDTH], w[:, v0:v0 + ATTN_WIDTH]], axis=1).T
        w_rest = jnp.concatenate([w[:, QK_WIDTH:v0], w[:, v0 + ATTN_WIDTH:]], axis=1)
        qt, k, vt, xr, gr, w_out16, w_up16, w_down16 = _in_proj(
            x, row(norm1_w[l]), wt, w_rest, q_norm_w[l].reshape(-1, 1),
            row(jnp.tile(k_norm_w[l], 2)), ct, st, rc, rs1, rs2,
            (w_out[l], w_up[l], w_down[l]))
        attn = _attention(qt, k, vt, row(lambda_q1[l]), row(lambda_k1[l]), row(lambda_q2[l]),
                          row(lambda_k2[l]), row(subln_w[l]), lambda_init)
        y = _rglru(xr, gr, conv_rnn_w[l], row(conv_rnn_b[l]),
                   _block_diag(w_gate_a[l]).astype(BF16), row(b_gate_a[l]),
                   _block_diag(w_gate_x[l]).astype(BF16), row(b_gate_x[l]),
                   row(lru_lambda[l]), row(rnn_norm_w[l]))
        x = _mlp(attn, y, x, w_out16, row(norm2_w[l]), w_up16,
                 conv_ffn_w[l], row(conv_ffn_b[l]), w_down16)
    return x
```

```python
import functools
import math

import jax
import jax.numpy as jnp
import numpy as np
from jax import lax
from jax.experimental import pallas as pl
from jax.experimental.pallas import tpu as pltpu

D_MODEL = 1024
DEPTH = 1
ATTN_HEADS = 4
QK_HEAD_DIM = 64
V_HEAD_DIM = 2 * QK_HEAD_DIM
ATTN_WIDTH = ATTN_HEADS * V_HEAD_DIM
QK_WIDTH = ATTN_HEADS * 2 * QK_HEAD_DIM
ROT_DIM = QK_HEAD_DIM // 4
ROPE_THETA = 500000.0
RNN_WIDTH = D_MODEL - ATTN_WIDTH
RNN_HEADS = 8
RNN_BLOCK = RNN_WIDTH // RNN_HEADS
RNN_CONV_WIDTH = 4
LRU_C = 8.0
D_FF = 3 * D_MODEL
FFN_CONV_WIDTH = 3
EPS = 1e-6

LANES = 128
SUBLANES_F32 = 8
SUBLANES_BF16 = 16
VMEM_LIMIT_BYTES = 56 * 1024 * 1024

PROJ_ROWS = 1024
ATTN_BLOCK = 512
ATTN_GROUP = 4
RNN_ROWS = 1024
FFN_ROWS = 512

LOG2E = math.log2(math.e)
F32 = jnp.float32
BF16 = jnp.bfloat16


def _gelu_gate(x, half_v):
    c = math.sqrt(2.0 / math.pi)
    t = jnp.tanh(x * (c + (c * 0.044715) * (x * x)))
    h = x * half_v
    return h + h * t


def _in_proj_kernel(x_ref, n1_ref, wt_ref, w_ref, qw_ref, kw_ref, ct_ref, st_ref, rc_ref,
                    rs1_ref, rs2_ref, wo_ref, wu_ref, wd_ref,
                    qt_ref, k_ref, vt_ref, xr_ref, gr_ref, wo16_ref, wu16_ref, wd16_ref):
    wo16_ref[...] = wo_ref[...].astype(BF16)
    wu16_ref[...] = wu_ref[...].astype(BF16)
    wd16_ref[...] = wd_ref[...].astype(BF16)

    x = x_ref[0]
    ms = jnp.mean(x * x, axis=-1, keepdims=True)
    h = (x * lax.rsqrt(ms + EPS) * n1_ref[...]).astype(BF16)

    half = ROT_DIM // 2
    lane = lax.broadcasted_iota(jnp.int32, (1, LANES), 1)
    lo = lane < QK_HEAD_DIM
    rc, rs1, rs2 = rc_ref[...], rs1_ref[...], rs2_ref[...]
    pk = jnp.dot(h, w_ref[:, 0:QK_WIDTH], preferred_element_type=F32)
    kw = kw_ref[...]
    for hd in range(ATTN_HEADS):
        y = pk[:, LANES * hd:LANES * (hd + 1)]
        sq = y * y
        s_all = jnp.sum(sq, axis=-1, keepdims=True)
        s_lo = jnp.sum(jnp.where(lo, sq, 0.0), axis=-1, keepdims=True)
        msq = jnp.where(lo, s_lo, s_all - s_lo) * (1.0 / QK_HEAD_DIM)
        y = y * lax.rsqrt(msq + EPS) * kw
        up = pltpu.roll(y, LANES - half, axis=1)
        dn = pltpu.roll(y, half, axis=1)
        k_ref[0, :, LANES * hd:LANES * (hd + 1)] = (y * rc + up * rs1 + dn * rs2).astype(BF16)
    c = QK_WIDTH
    xr_ref[0] = jnp.dot(h, w_ref[:, c:c + RNN_WIDTH], preferred_element_type=F32)
    c += RNN_WIDTH
    gr_ref[0] = jnp.dot(h, w_ref[:, c:c + RNN_WIDTH], preferred_element_type=F32)

    nt = (((1,), (1,)), ((), ()))
    pq = lax.dot_general(wt_ref[0:QK_WIDTH, :], h, nt, preferred_element_type=F32)
    ct, st = ct_ref[...], st_ref[...]
    q_scale = LOG2E * QK_HEAD_DIM ** -0.5
    for blk in range(QK_WIDTH // QK_HEAD_DIM):
        r0 = blk * QK_HEAD_DIM
        z = pq[r0:r0 + QK_HEAD_DIM, :]
        msq = jnp.mean(z * z, axis=0, keepdims=True)
        z = z * lax.rsqrt(msq + EPS) * qw_ref[...]
        x1, x2 = z[0:half, :], z[half:ROT_DIM, :]
        z = jnp.concatenate([x1 * ct - x2 * st, x2 * ct + x1 * st, z[ROT_DIM:, :]], axis=0)
        qt_ref[0, r0:r0 + QK_HEAD_DIM, :] = (z * q_scale).astype(BF16)
    vt_ref[0] = lax.dot_general(wt_ref[QK_WIDTH:, :], h, nt,
                                preferred_element_type=F32).astype(BF16)


def _in_proj(x, n1, wt, w_rest, qw_col, kw2, ct, st, rc, rs1, rs2, later_weights):
    b, s, d = x.shape
    tm = PROJ_ROWS
    n_steps = b * (s // tm)

    def slab(w):
        rows, cols = w.shape
        assert rows % (n_steps * SUBLANES_BF16) == 0
        return pl.BlockSpec((rows // n_steps, cols), lambda bi, i: (bi * (s // tm) + i, 0))

    row_blk = lambda w: pl.BlockSpec((1, tm, w), lambda bi, i: (bi, i, 0))
    col_blk = lambda w: pl.BlockSpec((1, w, tm), lambda bi, i: (bi, 0, i))
    const = lambda shape: pl.BlockSpec(shape, lambda bi, i: (0,) * len(shape))
    rope_blk = pl.BlockSpec((tm, LANES), lambda bi, i: (i, 0))
    rope_t_blk = pl.BlockSpec((ROT_DIM // 2, tm), lambda bi, i: (0, i))
    return pl.pallas_call(
        _in_proj_kernel,
        grid=(b, s // tm),
        in_specs=[row_blk(d), const((1, d)), const(wt.shape), const(w_rest.shape),
                  const((QK_HEAD_DIM, 1)), const((1, LANES)), rope_t_blk, rope_t_blk,
                  rope_blk, rope_blk, rope_blk] + [slab(w) for w in later_weights],
        out_specs=[col_blk(QK_WIDTH), row_blk(QK_WIDTH), col_blk(ATTN_WIDTH),
                   row_blk(RNN_WIDTH), row_blk(RNN_WIDTH)] + [slab(w) for w in later_weights],
        out_shape=[jax.ShapeDtypeStruct((b, QK_WIDTH, s), BF16),
                   jax.ShapeDtypeStruct((b, s, QK_WIDTH), BF16),
                   jax.ShapeDtypeStruct((b, ATTN_WIDTH, s), BF16),
                   jax.ShapeDtypeStruct((b, s, RNN_WIDTH), F32),
                   jax.ShapeDtypeStruct((b, s, RNN_WIDTH), F32)]
        + [jax.ShapeDtypeStruct(w.shape, BF16) for w in later_weights],
        compiler_params=pltpu.CompilerParams(
            dimension_semantics=("arbitrary", "arbitrary"),
            vmem_limit_bytes=VMEM_LIMIT_BYTES),
        name="in_proj",
    )(x, n1, wt, w_rest, qw_col, kw2, ct, st, rc, rs1, rs2, *later_weights)


def _attn_kernel(qlo_ref, qhi_ref, k_ref, vt_ref, lq1_ref, lk1_ref, lq2_ref, lk2_ref, sw_ref,
                 o_ref, qz_ref, m_ref, acc_ref, sa_ref, mxa_ref, sb_ref,
                 mxb_ref, *, lambda_init, n_blocks):
    t = ATTN_BLOCK
    lo = pl.program_id(2)
    hi = n_blocks - 1 - lo
    dim = lax.broadcasted_iota(jnp.int32, (LANES, 1), 0)
    for w, q_ref in enumerate((qlo_ref, qhi_ref)):
        qt = q_ref[0]
        zero = jnp.zeros_like(qt)
        qz_ref[w, :, 0:t] = jnp.where(dim < QK_HEAD_DIM, qt, zero)
        qz_ref[w, :, t:2 * t] = jnp.where(dim >= QK_HEAD_DIM, qt, zero)

    m_ref[...] = jnp.full(m_ref.shape, -1e30, F32)
    acc_ref[...] = jnp.zeros(acc_ref.shape, F32)

    def scores(buf, w, j, masked):
        s_ref, mx_ref = buf
        r0 = pl.multiple_of(j * t, t)
        kb = k_ref[0, pl.ds(r0, t), :]
        s = jnp.dot(kb, qz_ref[w], preferred_element_type=F32)
        if masked:
            key = lax.broadcasted_iota(jnp.int32, (t, 2 * t), 0)
            qry = lax.broadcasted_iota(jnp.int32, (t, 2 * t), 1) & (t - 1)
            s = jnp.where(key <= qry, s, -jnp.inf)
        s_ref[...] = s
        mx_ref[...] = jnp.max(s, axis=0, keepdims=True)

    def accumulate(buf, w, j):
        s_ref, mx_ref = buf
        r0 = pl.multiple_of(j * t, t)
        vtb = jnp.concatenate([vt_ref[0, :, pl.ds(r0, t)],
                               jnp.ones((SUBLANES_BF16, t), BF16)], axis=0)
        m_prev = m_ref[w]
        m_new = jnp.maximum(m_prev, mx_ref[...])
        alpha = jnp.exp2(m_prev - m_new)
        e = jnp.exp2(s_ref[...] - m_new)
        acc_ref[w] = alpha * acc_ref[w] + jnp.dot(vtb, e.astype(BF16),
                                                  preferred_element_type=F32)
        m_ref[w] = m_new

    n_items = n_blocks + 1
    group = ATTN_GROUP
    assert (n_items - 1) % group == 0

    def item(n):
        if isinstance(n, int) and n < 2:
            return (n, (lo, hi)[n])
        is_hi = (n - 2) >= lo
        return (is_hi.astype(jnp.int32), jnp.where(is_hi, n - 2 - lo, n - 2))

    bufs = ((sa_ref, mxa_ref), (sb_ref, mxb_ref))

    def run_group(first, masked_until):
        for r in range(group):
            n = first + r
            scores(bufs[(r + 1) % 2], *item(n + 1),
                   isinstance(n, int) and n + 1 < masked_until)
            accumulate(bufs[r % 2], *item(n))

    scores(bufs[0], *item(0), True)
    run_group(0, 2)

    def trip(i, carry):
        run_group(i * group, 0)
        return carry

    lax.fori_loop(1, (n_items - 1) // group, trip, 0)
    accumulate(bufs[0], *item(n_items - 1))

    lam = (jnp.exp(jnp.sum(lq1_ref[...] * lk1_ref[...], axis=-1, keepdims=True))
           - jnp.exp(jnp.sum(lq2_ref[...] * lk2_ref[...], axis=-1, keepdims=True))
           + lambda_init)
    dv = V_HEAD_DIM
    for w in range(2):
        l0 = acc_ref[w, dv:dv + 1, 0:t]
        l1 = acc_ref[w, dv:dv + 1, t:2 * t]
        o = acc_ref[w, 0:dv, 0:t] * (1.0 / l0) - acc_ref[w, 0:dv, t:2 * t] * (lam / l1)
        ms = jnp.mean(o * o, axis=0, keepdims=True)
        o = o * (lax.rsqrt(ms + EPS) * (1.0 - lambda_init))
        o_ref[0, w] = (o.T * sw_ref[...]).astype(BF16)


def _attention(qt, k, vt, lq1, lk1, lq2, lk2, sw, lambda_init):
    b, s, _ = k.shape
    t = ATTN_BLOCK
    nb = s // t
    half = nb // 2
    vec = lambda n: pl.BlockSpec((1, n), lambda bi, h, p: (0, 0))
    return pl.pallas_call(
        functools.partial(_attn_kernel, lambda_init=lambda_init, n_blocks=nb),
        grid=(b, ATTN_HEADS, half),
        in_specs=[pl.BlockSpec((1, LANES, t), lambda bi, h, p: (bi, h, p)),
                  pl.BlockSpec((1, LANES, t), lambda bi, h, p: (bi, h, nb - 1 - p)),
                  pl.BlockSpec((1, s, LANES), lambda bi, h, p: (bi, 0, h)),
                  pl.BlockSpec((1, LANES, s), lambda bi, h, p: (bi, h, 0)),
                  vec(QK_HEAD_DIM), vec(QK_HEAD_DIM), vec(QK_HEAD_DIM), vec(QK_HEAD_DIM),
                  vec(V_HEAD_DIM)],
        out_specs=pl.BlockSpec((1, 2, t, LANES), lambda bi, h, p: (bi, 0, p, h)),
        out_shape=jax.ShapeDtypeStruct((b, 2, s // 2, ATTN_WIDTH), BF16),
        scratch_shapes=[pltpu.VMEM((2, LANES, 2 * t), BF16),
                        pltpu.VMEM((2, 1, 2 * t), F32),
                        pltpu.VMEM((2, V_HEAD_DIM + SUBLANES_BF16, 2 * t), F32),
                        pltpu.VMEM((t, 2 * t), F32), pltpu.VMEM((1, 2 * t), F32),
                        pltpu.VMEM((t, 2 * t), F32), pltpu.VMEM((1, 2 * t), F32)],
        compiler_params=pltpu.CompilerParams(
            dimension_semantics=("arbitrary", "arbitrary", "arbitrary"),
            vmem_limit_bytes=VMEM_LIMIT_BYTES),
        name="diff_attn",
    )(qt, qt, k, vt, lq1, lk1, lq2, lk2, sw)


def _rglru_kernel(xr_ref, gr_ref, cw_ref, cb_ref, wa_ref, ba_ref, wx_ref, bx_ref, lam_ref,
                  nw_ref, y_ref, ext_ref, a_ref, b_ref, hc_ref):
    ts = RNN_ROWS
    halo = SUBLANES_F32

    @pl.when(pl.program_id(1) == 0)
    def _():
        ext_ref[0:halo, :] = jnp.zeros((halo, RNN_WIDTH), F32)
        hc_ref[...] = jnp.zeros((halo, RNN_WIDTH), F32)

    x = xr_ref[0]
    ext_ref[halo:halo + ts, :] = x
    xc = cb_ref[...] + cw_ref[RNN_CONV_WIDTH - 1:RNN_CONV_WIDTH, :] * x
    for d in range(1, RNN_CONV_WIDTH):
        tap = RNN_CONV_WIDTH - 1 - d
        xc = xc + cw_ref[tap:tap + 1, :] * ext_ref[halo - d:halo - d + ts, :]
    ext_ref[0:halo, :] = x[ts - halo:ts, :]

    xb = xc.astype(BF16)
    r = jax.nn.sigmoid(jnp.dot(xb, wa_ref[...], preferred_element_type=F32) + ba_ref[...])
    ig = jax.nn.sigmoid(jnp.dot(xb, wx_ref[...], preferred_element_type=F32) + bx_ref[...])
    z = -lam_ref[...]
    softplus = jnp.maximum(z, 0.0) + jnp.log1p(jnp.exp(-jnp.abs(z)))
    log_a = r * (-LRU_C * softplus)
    a = jnp.exp(log_a)
    w = 1.0 - a * a
    mult = jnp.where(w > 0.0, w * lax.rsqrt(w), 0.0)
    u = mult * ig * xc

    tiles = (ts // SUBLANES_F32, SUBLANES_F32, RNN_WIDTH)
    a = a.reshape(tiles)
    u = u.reshape(tiles)
    sub = lax.broadcasted_iota(jnp.int32, (1, SUBLANES_F32, 1), 1)
    d = 1
    while d < SUBLANES_F32:
        keep = sub >= d
        a_prev = jnp.where(keep, pltpu.roll(a, d, axis=1), 1.0)
        u_prev = jnp.where(keep, pltpu.roll(u, d, axis=1), 0.0)
        u = a * u_prev + u
        a = a * a_prev
        d *= 2
    h_prev = hc_ref[...]
    hs = []
    for j in range(ts // SUBLANES_F32):
        h = u[j] + a[j] * h_prev
        hs.append(h)
        h_prev = jnp.broadcast_to(h[SUBLANES_F32 - 1:SUBLANES_F32, :], h.shape)
    hc_ref[...] = h_prev

    y = _gelu_gate(gr_ref[0], 0.5 * jnp.concatenate(hs, axis=0))
    ms = jnp.mean(y * y, axis=-1, keepdims=True)
    y_ref[0] = (y * lax.rsqrt(ms + EPS) * nw_ref[...]).astype(BF16)


def _rglru(xr, gr, cw, cb, wa, ba, wx, bx, lam, nw):
    b, s, c = xr.shape
    ts = RNN_ROWS
    row_blk = pl.BlockSpec((1, ts, c), lambda bi, i: (bi, i, 0))
    const = lambda shape: pl.BlockSpec(shape, lambda bi, i: (0,) * len(shape))
    return pl.pallas_call(
        _rglru_kernel,
        grid=(b, s // ts),
        in_specs=[row_blk, row_blk, const((RNN_CONV_WIDTH, c)), const((1, c)),
                  const((c, c)), const((1, c)), const((c, c)), const((1, c)),
                  const((1, c)), const((1, c))],
        out_specs=row_blk,
        out_shape=jax.ShapeDtypeStruct((b, s, c), BF16),
        scratch_shapes=[pltpu.VMEM((ts + SUBLANES_F32, c), F32),
                        pltpu.VMEM((ts, c), F32), pltpu.VMEM((ts, c), F32),
                        pltpu.VMEM((SUBLANES_F32, c), F32)],
        compiler_params=pltpu.CompilerParams(
            dimension_semantics=("arbitrary", "arbitrary"),
            vmem_limit_bytes=VMEM_LIMIT_BYTES),
        name="rglru",
    )(xr, gr, cw, cb, wa, ba, wx, bx, lam, nw)


def _mlp_kernel(attn_ref, y_ref, x_ref, wo_ref, n2_ref, wu_ref, cw_ref, cb_ref, wd_ref,
                o_ref, ug_ref, uv_ref):
    tm = FFN_ROWS
    hr = SUBLANES_F32

    @pl.when(pl.program_id(1) == 0)
    def _():
        ug_ref[0:hr, :] = jnp.zeros((hr, D_FF), F32)
        uv_ref[0:hr, :] = jnp.zeros((hr, D_FF), F32)

    x1 = x_ref[0] + jnp.dot(attn_ref[0, 0], wo_ref[0:ATTN_WIDTH, :],
                            preferred_element_type=F32)
    x1 = x1 + jnp.dot(y_ref[0], wo_ref[ATTN_WIDTH:, :], preferred_element_type=F32)
    ms = jnp.mean(x1 * x1, axis=-1, keepdims=True)
    h2 = (x1 * lax.rsqrt(ms + EPS) * n2_ref[...]).astype(BF16)

    def conv(u_ref, c0, scale=1.0):
        cw = cw_ref[:, c0:c0 + D_FF] * scale
        y = cb_ref[:, c0:c0 + D_FF] * scale + (
            cw[FFN_CONV_WIDTH - 1:FFN_CONV_WIDTH, :] * u_ref[hr:hr + tm, :])
        for d in range(1, FFN_CONV_WIDTH):
            tap = FFN_CONV_WIDTH - 1 - d
            y = y + cw[tap:tap + 1, :] * u_ref[hr - d:hr - d + tm, :]
        return y

    ug_ref[hr:hr + tm, :] = jnp.dot(h2, wu_ref[:, 0:D_FF], preferred_element_type=F32)
    uv_ref[hr:hr + tm, :] = jnp.dot(h2, wu_ref[:, D_FF:], preferred_element_type=F32)
    act = _gelu_gate(conv(ug_ref, 0), conv(uv_ref, D_FF, 0.5)).astype(BF16)
    ug_ref[0:hr, :] = ug_ref[tm:tm + hr, :]
    uv_ref[0:hr, :] = uv_ref[tm:tm + hr, :]
    o_ref[0] = x1 + jnp.dot(act, wd_ref[...], preferred_element_type=F32)


def _mlp(attn, y, x, w_out, n2, w_up, cw, cb, w_down):
    b, s, d = x.shape
    tm = FFN_ROWS
    hr = SUBLANES_F32
    row_blk = lambda w: pl.BlockSpec((1, tm, w), lambda bi, i: (bi, i, 0))
    ratio = ATTN_BLOCK // tm
    nq = s // ATTN_BLOCK
    nh = nq // 2

    def fold(i):
        qb = i // ratio
        return qb // nh, jnp.where(qb < nh, qb, nq - 1 - qb) * ratio + i % ratio

    attn_blk = pl.BlockSpec((1, 1, tm, ATTN_WIDTH), lambda bi, i: (bi, *fold(i), 0))
    resident = lambda shape: pl.BlockSpec(shape, lambda bi, i: (0,) * len(shape),
                                          pipeline_mode=pl.Buffered(1))
    return pl.pallas_call(
        _mlp_kernel,
        grid=(b, s // tm),
        in_specs=[attn_blk, row_blk(RNN_WIDTH), row_blk(d),
                  resident((d, d)), resident((1, d)), resident((d, 2 * D_FF)),
                  resident((FFN_CONV_WIDTH, 2 * D_FF)), resident((1, 2 * D_FF)),
                  resident((D_FF, d))],
        out_specs=row_blk(d),
        out_shape=jax.ShapeDtypeStruct((b, s, d), F32),
        scratch_shapes=[pltpu.VMEM((hr + tm, D_FF), F32)] * 2,
        compiler_params=pltpu.CompilerParams(
            dimension_semantics=("arbitrary", "arbitrary"),
            vmem_limit_bytes=VMEM_LIMIT_BYTES),
        name="mlp",
    )(attn, y, x, w_out, n2, w_up, cw, cb, w_down)


def _rope_tables(seq):
    f32 = np.float32
    pos = np.arange(seq, dtype=f32)
    inv_freq = f32(ROPE_THETA) ** (-(np.arange(0, ROT_DIM, 2, dtype=f32) / f32(ROT_DIM)))
    ang = pos[:, None] * inv_freq[None, :].astype(f32)
    cos, sin = np.cos(ang).astype(f32), np.sin(ang).astype(f32)
    half = ROT_DIM // 2
    pad = QK_HEAD_DIM - ROT_DIM
    one = np.ones((seq, pad), f32)
    zero_h = np.zeros((seq, half), f32)
    zero_p = np.zeros((seq, pad), f32)
    rc = np.concatenate([cos, cos, one], axis=1)
    rs1 = np.concatenate([-sin, zero_h, zero_p], axis=1)
    rs2 = np.concatenate([zero_h, sin, zero_p], axis=1)
    tables = (cos.T, sin.T) + tuple(np.tile(t, (1, 2)) for t in (rc, rs1, rs2))
    return tuple(jnp.asarray(np.ascontiguousarray(t)) for t in tables)


def _block_diag(w):
    h, i, j = w.shape
    eye = jnp.eye(h, dtype=w.dtype)
    return (eye[:, None, :, None] * w[:, :, None, :]).reshape(h * i, h * j)


def kernel(x, norm1_w, w_in, q_norm_w, k_norm_w, lambda_q1, lambda_k1, lambda_q2, lambda_k2,
           subln_w, conv_rnn_w, conv_rnn_b, w_gate_a, b_gate_a, w_gate_x, b_gate_x, lru_lambda,
           rnn_norm_w, w_out, norm2_w, w_up, conv_ffn_w, conv_ffn_b, w_down):
    b, s, d = x.shape
    ct, st, rc, rs1, rs2 = _rope_tables(s)
    row = lambda v: v.reshape(1, -1)
    for l in range(DEPTH):
        lambda_init = 0.8 - 0.6 * math.exp(-0.3 * l)
        w = w_in[l].astype(BF16)
        v0 = 2 * QK_WIDTH
        wt = jnp.concatenate([w[:, :QK_WIDTH], w[:, v0:v0 + ATTN_WIDTH]], axis=1).T
        w_rest = jnp.concatenate([w[:, QK_WIDTH:v0], w[:, v0 + ATTN_WIDTH:]], axis=1)
        qt, k, vt, xr, gr, w_out16, w_up16, w_down16 = _in_proj(
            x, row(norm1_w[l]), wt, w_rest, q_norm_w[l].reshape(-1, 1),
            row(jnp.tile(k_norm_w[l], 2)), ct, st, rc, rs1, rs2,
            (w_out[l], w_up[l], w_down[l]))
        attn = _attention(qt, k, vt, row(lambda_q1[l]), row(lambda_k1[l]), row(lambda_q2[l]),
                          row(lambda_k2[l]), row(subln_w[l]), lambda_init)
        y = _rglru(xr, gr, conv_rnn_w[l], row(conv_rnn_b[l]),
                   _block_diag(w_gate_a[l]).astype(BF16), row(b_gate_a[l]),
                   _block_diag(w_gate_x[l]).astype(BF16), row(b_gate_x[l]),
                   row(lru_lambda[l]), row(rnn_norm_w[l]))
        x = _mlp(attn, y, x, w_out16, row(norm2_w[l]), w_up16,
                 conv_ffn_w[l], row(conv_ffn_b[l]), w_down16)
    return x
```

```python
import functools
import math

import jax
import jax.numpy as jnp
import numpy as np
from jax import lax
from jax.experimental import pallas as pl
from jax.experimental.pallas import tpu as pltpu

D_MODEL = 1024
DEPTH = 1
ATTN_HEADS = 4
QK_HEAD_DIM = 64
V_HEAD_DIM = 2 * QK_HEAD_DIM
ATTN_WIDTH = ATTN_HEADS * V_HEAD_DIM
QK_WIDTH = ATTN_HEADS * 2 * QK_HEAD_DIM
ROT_DIM = QK_HEAD_DIM // 4
ROPE_THETA = 500000.0
RNN_WIDTH = D_MODEL - ATTN_WIDTH
RNN_HEADS = 8
RNN_BLOCK = RNN_WIDTH // RNN_HEADS
RNN_CONV_WIDTH = 4
LRU_C = 8.0
D_FF = 3 * D_MODEL
FFN_CONV_WIDTH = 3
EPS = 1e-6

LANES = 128
SUBLANES_F32 = 8
SUBLANES_BF16 = 16
VMEM_LIMIT_BYTES = 56 * 1024 * 1024

PROJ_ROWS = 1024
ATTN_BLOCK = 512
ATTN_GROUP = 4
RNN_ROWS = 1024
FFN_ROWS = 512

LOG2E = math.log2(math.e)
F32 = jnp.float32
BF16 = jnp.bfloat16


def _gelu_gate(x, half_v):
    c = math.sqrt(2.0 / math.pi)
    t = jnp.tanh(x * (c + (c * 0.044715) * (x * x)))
    h = x * half_v
    return h + h * t


def _in_proj_kernel(x_ref, n1_ref, wt_ref, w_ref, qw_ref, kw_ref, ct_ref, st_ref, rc_ref,
                    rs1_ref, rs2_ref, wo_ref, wu_ref, wd_ref,
                    qt_ref, k_ref, vt_ref, xr_ref, gr_ref, wo16_ref, wu16_ref, wd16_ref):
    wo16_ref[...] = wo_ref[...].astype(BF16)
    wu16_ref[...] = wu_ref[...].astype(BF16)
    wd16_ref[...] = wd_ref[...].astype(BF16)

    x = x_ref[0]
    ms = jnp.mean(x * x, axis=-1, keepdims=True)
    h = (x * lax.rsqrt(ms + EPS) * n1_ref[...]).astype(BF16)

    half = ROT_DIM // 2
    lane = lax.broadcasted_iota(jnp.int32, (1, LANES), 1)
    lo = lane < QK_HEAD_DIM
    rc, rs1, rs2 = rc_ref[...], rs1_ref[...], rs2_ref[...]
    pk = jnp.dot(h, w_ref[:, 0:QK_WIDTH], preferred_element_type=F32)
    kw = kw_ref[...]
    for hd in range(ATTN_HEADS):
        y = pk[:, LANES * hd:LANES * (hd + 1)]
        sq = y * y
        s_all = jnp.sum(sq, axis=-1, keepdims=True)
        s_lo = jnp.sum(jnp.where(lo, sq, 0.0), axis=-1, keepdims=True)
        msq = jnp.where(lo, s_lo, s_all - s_lo) * (1.0 / QK_HEAD_DIM)
        y = y * lax.rsqrt(msq + EPS) * kw
        up = pltpu.roll(y, LANES - half, axis=1)
        dn = pltpu.roll(y, half, axis=1)
        k_ref[0, :, LANES * hd:LANES * (hd + 1)] = (y * rc + up * rs1 + dn * rs2).astype(BF16)
    c = QK_WIDTH
    xr_ref[0] = jnp.dot(h, w_ref[:, c:c + RNN_WIDTH], preferred_element_type=F32)
    c += RNN_WIDTH
    gr_ref[0] = jnp.dot(h, w_ref[:, c:c + RNN_WIDTH], preferred_element_type=F32)

    nt = (((1,), (1,)), ((), ()))
    pq = lax.dot_general(wt_ref[0:QK_WIDTH, :], h, nt, preferred_element_type=F32)
    ct, st = ct_ref[...], st_ref[...]
    q_scale = LOG2E * QK_HEAD_DIM ** -0.5
    for blk in range(QK_WIDTH // QK_HEAD_DIM):
        r0 = blk * QK_HEAD_DIM
        z = pq[r0:r0 + QK_HEAD_DIM, :]
        msq = jnp.mean(z * z, axis=0, keepdims=True)
        z = z * lax.rsqrt(msq + EPS) * qw_ref[...]
        x1, x2 = z[0:half, :], z[half:ROT_DIM, :]
        z = jnp.concatenate([x1 * ct - x2 * st, x2 * ct + x1 * st, z[ROT_DIM:, :]], axis=0)
        qt_ref[0, r0:r0 + QK_HEAD_DIM, :] = (z * q_scale).astype(BF16)
    vt_ref[0] = lax.dot_general(wt_ref[QK_WIDTH:, :], h, nt,
                                preferred_element_type=F32).astype(BF16)


def _in_proj(x, n1, wt, w_rest, qw_col, kw2, ct, st, rc, rs1, rs2, later_weights):
    b, s, d = x.shape
    tm = PROJ_ROWS
    n_steps = b * (s // tm)

    def slab(w):
        rows, cols = w.shape
        assert rows % (n_steps * SUBLANES_BF16) == 0
        return pl.BlockSpec((rows // n_steps, cols), lambda bi, i: (bi * (s // tm) + i, 0))

    row_blk = lambda w: pl.BlockSpec((1, tm, w), lambda bi, i: (bi, i, 0))
    col_blk = lambda w: pl.BlockSpec((1, w, tm), lambda bi, i: (bi, 0, i))
    const = lambda shape: pl.BlockSpec(shape, lambda bi, i: (0,) * len(shape))
    rope_blk = pl.BlockSpec((tm, LANES), lambda bi, i: (i, 0))
    rope_t_blk = pl.BlockSpec((ROT_DIM // 2, tm), lambda bi, i: (0, i))
    return pl.pallas_call(
        _in_proj_kernel,
        grid=(b, s // tm),
        in_specs=[row_blk(d), const((1, d)), const(wt.shape), const(w_rest.shape),
                  const((QK_HEAD_DIM, 1)), const((1, LANES)), rope_t_blk, rope_t_blk,
                  rope_blk, rope_blk, rope_blk] + [slab(w) for w in later_weights],
        out_specs=[col_blk(QK_WIDTH), row_blk(QK_WIDTH), col_blk(ATTN_WIDTH),
                   row_blk(RNN_WIDTH), row_blk(RNN_WIDTH)] + [slab(w) for w in later_weights],
        out_shape=[jax.ShapeDtypeStruct((b, QK_WIDTH, s), BF16),
                   jax.ShapeDtypeStruct((b, s, QK_WIDTH), BF16),
                   jax.ShapeDtypeStruct((b, ATTN_WIDTH, s), BF16),
                   jax.ShapeDtypeStruct((b, s, RNN_WIDTH), F32),
                   jax.ShapeDtypeStruct((b, s, RNN_WIDTH), F32)]
        + [jax.ShapeDtypeStruct(w.shape, BF16) for w in later_weights],
        compiler_params=pltpu.CompilerParams(
            dimension_semantics=("arbitrary", "arbitrary"),
            vmem_limit_bytes=VMEM_LIMIT_BYTES),
        name="in_proj",
    )(x, n1, wt, w_rest, qw_col, kw2, ct, st, rc, rs1, rs2, *later_weights)


def _attn_kernel(qlo_ref, qhi_ref, k_ref, vt_ref, lq1_ref, lk1_ref, lq2_ref, lk2_ref, sw_ref,
                 o_ref, qz_ref, m_ref, acc_ref, sa_ref, mxa_ref, sb_ref,
                 mxb_ref, *, lambda_init, n_blocks):
    t = ATTN_BLOCK
    lo = pl.program_id(2)
    hi = n_blocks - 1 - lo
    dim = lax.broadcasted_iota(jnp.int32, (LANES, 1), 0)
    for w, q_ref in enumerate((qlo_ref, qhi_ref)):
        qt = q_ref[0]
        zero = jnp.zeros_like(qt)
        qz_ref[w, :, 0:t] = jnp.where(dim < QK_HEAD_DIM, qt, zero)
        qz_ref[w, :, t:2 * t] = jnp.where(dim >= QK_HEAD_DIM, qt, zero)

    th = t // 2
    late = [slice(c * t + th, (c + 1) * t) for c in range(2)]

    def v_ones(j):
        r0 = pl.multiple_of(j * t, t)
        return jnp.concatenate([vt_ref[0, :, pl.ds(r0, t)],
                                jnp.ones((SUBLANES_BF16, t), BF16)], axis=0)

    def scores(buf, w, j, masked):
        s_ref, mx_ref = buf
        r0 = pl.multiple_of(j * t, t)
        kb = k_ref[0, pl.ds(r0, t), :]
        if not masked:
            s = jnp.dot(kb, qz_ref[w], preferred_element_type=F32)
            s_ref[...] = s
            mx_ref[...] = jnp.max(s, axis=0, keepdims=True)
            return
        key = lax.broadcasted_iota(jnp.int32, (th, 2 * t), 0)
        qry = lax.broadcasted_iota(jnp.int32, (th, 2 * t), 1) & (t - 1)
        top = jnp.dot(kb[0:th, :], qz_ref[w], preferred_element_type=F32)
        top = jnp.where(key <= qry, top, -jnp.inf)
        s_ref[0:th, :] = top
        mx = jnp.max(top, axis=0, keepdims=True)
        tri = (lax.broadcasted_iota(jnp.int32, (th, th), 0)
               <= lax.broadcasted_iota(jnp.int32, (th, th), 1))
        pieces = []
        for c in range(2):
            bot = jnp.dot(kb[th:t, :], qz_ref[w, :, late[c]], preferred_element_type=F32)
            bot = jnp.where(tri, bot, -jnp.inf)
            s_ref[th:t, late[c]] = bot
            pieces += [mx[:, c * t:c * t + th],
                       jnp.maximum(mx[:, late[c]], jnp.max(bot, axis=0, keepdims=True))]
        mx_ref[...] = jnp.concatenate(pieces, axis=1)

    def accumulate_diag(buf, w, j):
        s_ref, mx_ref = buf
        vtb = v_ones(j)
        m_new = mx_ref[...]
        e_top = jnp.exp2(s_ref[0:th, :] - m_new)
        acc_ref[w] = jnp.dot(vtb[:, 0:th], e_top.astype(BF16), preferred_element_type=F32)
        for c in range(2):
            e_bot = jnp.exp2(s_ref[th:t, late[c]] - m_new[:, late[c]])
            acc_ref[w, :, late[c]] += jnp.dot(vtb[:, th:t], e_bot.astype(BF16),
                                              preferred_element_type=F32)
        m_ref[w] = m_new

    def accumulate(buf, w, j):
        s_ref, mx_ref = buf
        vtb = v_ones(j)
        m_prev = m_ref[w]
        m_new = jnp.maximum(m_prev, mx_ref[...])
        alpha = jnp.exp2(m_prev - m_new)
        e = jnp.exp2(s_ref[...] - m_new)
        acc_ref[w] = alpha * acc_ref[w] + jnp.dot(vtb, e.astype(BF16),
                                                  preferred_element_type=F32)
        m_ref[w] = m_new

    n_items = n_blocks + 1
    group = ATTN_GROUP
    assert (n_items - 1) % group == 0

    def item(n):
        if isinstance(n, int) and n < 2:
            return (n, (lo, hi)[n])
        is_hi = (n - 2) >= lo
        return (is_hi.astype(jnp.int32), jnp.where(is_hi, n - 2 - lo, n - 2))

    bufs = ((sa_ref, mxa_ref), (sb_ref, mxb_ref))

    def run_group(first, masked_until):
        for r in range(group):
            n = first + r
            scores(bufs[(r + 1) % 2], *item(n + 1),
                   isinstance(n, int) and n + 1 < masked_until)
            if isinstance(n, int) and n < masked_until:
                accumulate_diag(bufs[r % 2], *item(n))
            else:
                accumulate(bufs[r % 2], *item(n))

    scores(bufs[0], *item(0), True)
    run_group(0, 2)

    def trip(i, carry):
        run_group(i * group, 0)
        return carry

    lax.fori_loop(1, (n_items - 1) // group, trip, 0)
    accumulate(bufs[0], *item(n_items - 1))

    lam = (jnp.exp(jnp.sum(lq1_ref[...] * lk1_ref[...], axis=-1, keepdims=True))
           - jnp.exp(jnp.sum(lq2_ref[...] * lk2_ref[...], axis=-1, keepdims=True))
           + lambda_init)
    dv = V_HEAD_DIM
    for w in range(2):
        l0 = acc_ref[w, dv:dv + 1, 0:t]
        l1 = acc_ref[w, dv:dv + 1, t:2 * t]
        o = acc_ref[w, 0:dv, 0:t] * (1.0 / l0) - acc_ref[w, 0:dv, t:2 * t] * (lam / l1)
        ms = jnp.mean(o * o, axis=0, keepdims=True)
        o = o * (lax.rsqrt(ms + EPS) * (1.0 - lambda_init))
        o_ref[0, w] = (o.T * sw_ref[...]).astype(BF16)


def _attention(qt, k, vt, lq1, lk1, lq2, lk2, sw, lambda_init):
    b, s, _ = k.shape
    t = ATTN_BLOCK
    nb = s // t
    half = nb // 2
    vec = lambda n: pl.BlockSpec((1, n), lambda bi, h, p: (0, 0))
    return pl.pallas_call(
        functools.partial(_attn_kernel, lambda_init=lambda_init, n_blocks=nb),
        grid=(b, ATTN_HEADS, half),
        in_specs=[pl.BlockSpec((1, LANES, t), lambda bi, h, p: (bi, h, p)),
                  pl.BlockSpec((1, LANES, t), lambda bi, h, p: (bi, h, nb - 1 - p)),
                  pl.BlockSpec((1, s, LANES), lambda bi, h, p: (bi, 0, h)),
                  pl.BlockSpec((1, LANES, s), lambda bi, h, p: (bi, h, 0)),
                  vec(QK_HEAD_DIM), vec(QK_HEAD_DIM), vec(QK_HEAD_DIM), vec(QK_HEAD_DIM),
                  vec(V_HEAD_DIM)],
        out_specs=pl.BlockSpec((1, 2, t, LANES), lambda bi, h, p: (bi, 0, p, h)),
        out_shape=jax.ShapeDtypeStruct((b, 2, s // 2, ATTN_WIDTH), BF16),
        scratch_shapes=[pltpu.VMEM((2, LANES, 2 * t), BF16),
                        pltpu.VMEM((2, 1, 2 * t), F32),
                        pltpu.VMEM((2, V_HEAD_DIM + SUBLANES_BF16, 2 * t), F32),
                        pltpu.VMEM((t, 2 * t), F32), pltpu.VMEM((1, 2 * t), F32),
                        pltpu.VMEM((t, 2 * t), F32), pltpu.VMEM((1, 2 * t), F32)],
        compiler_params=pltpu.CompilerParams(
            dimension_semantics=("arbitrary", "arbitrary", "arbitrary"),
            vmem_limit_bytes=VMEM_LIMIT_BYTES),
        name="diff_attn",
    )(qt, qt, k, vt, lq1, lk1, lq2, lk2, sw)


def _rglru_kernel(xr_ref, gr_ref, cw_ref, cb_ref, wa_ref, ba_ref, wx_ref, bx_ref, lam_ref,
                  nw_ref, y_ref, ext_ref, a_ref, b_ref, hc_ref):
    ts = RNN_ROWS
    halo = SUBLANES_F32

    @pl.when(pl.program_id(1) == 0)
    def _():
        ext_ref[0:halo, :] = jnp.zeros((halo, RNN_WIDTH), F32)
        hc_ref[...] = jnp.zeros((halo, RNN_WIDTH), F32)

    x = xr_ref[0]
    ext_ref[halo:halo + ts, :] = x
    xc = cb_ref[...] + cw_ref[RNN_CONV_WIDTH - 1:RNN_CONV_WIDTH, :] * x
    for d in range(1, RNN_CONV_WIDTH):
        tap = RNN_CONV_WIDTH - 1 - d
        xc = xc + cw_ref[tap:tap + 1, :] * ext_ref[halo - d:halo - d + ts, :]
    ext_ref[0:halo, :] = x[ts - halo:ts, :]

    xb = xc.astype(BF16)
    r = jax.nn.sigmoid(jnp.dot(xb, wa_ref[...], preferred_element_type=F32) + ba_ref[...])
    ig = jax.nn.sigmoid(jnp.dot(xb, wx_ref[...], preferred_element_type=F32) + bx_ref[...])
    z = -lam_ref[...]
    softplus = jnp.maximum(z, 0.0) + jnp.log1p(jnp.exp(-jnp.abs(z)))
    log_a = r * (-LRU_C * softplus)
    a = jnp.exp(log_a)
    w = 1.0 - a * a
    mult = jnp.where(w > 0.0, w * lax.rsqrt(w), 0.0)
    u = mult * ig * xc

    tiles = (ts // SUBLANES_F32, SUBLANES_F32, RNN_WIDTH)
    a = a.reshape(tiles)
    u = u.reshape(tiles)
    sub = lax.broadcasted_iota(jnp.int32, (1, SUBLANES_F32, 1), 1)
    d = 1
    while d < SUBLANES_F32:
        keep = sub >= d
        a_prev = jnp.where(keep, pltpu.roll(a, d, axis=1), 1.0)
        u_prev = jnp.where(keep, pltpu.roll(u, d, axis=1), 0.0)
        u = a * u_prev + u
        a = a * a_prev
        d *= 2
    h_prev = hc_ref[...]
    hs = []
    for j in range(ts // SUBLANES_F32):
        h = u[j] + a[j] * h_prev
        hs.append(h)
        h_prev = jnp.broadcast_to(h[SUBLANES_F32 - 1:SUBLANES_F32, :], h.shape)
    hc_ref[...] = h_prev

    y = _gelu_gate(gr_ref[0], 0.5 * jnp.concatenate(hs, axis=0))
    ms = jnp.mean(y * y, axis=-1, keepdims=True)
    y_ref[0] = (y * lax.rsqrt(ms + EPS) * nw_ref[...]).astype(BF16)


def _rglru(xr, gr, cw, cb, wa, ba, wx, bx, lam, nw):
    b, s, c = xr.shape
    ts = RNN_ROWS
    row_blk = pl.BlockSpec((1, ts, c), lambda bi, i: (bi, i, 0))
    const = lambda shape: pl.BlockSpec(shape, lambda bi, i: (0,) * len(shape))
    return pl.pallas_call(
        _rglru_kernel,
        grid=(b, s // ts),
        in_specs=[row_blk, row_blk, const((RNN_CONV_WIDTH, c)), const((1, c)),
                  const((c, c)), const((1, c)), const((c, c)), const((1, c)),
                  const((1, c)), const((1, c))],
        out_specs=row_blk,
        out_shape=jax.ShapeDtypeStruct((b, s, c), BF16),
        scratch_shapes=[pltpu.VMEM((ts + SUBLANES_F32, c), F32),
                        pltpu.VMEM((ts, c), F32), pltpu.VMEM((ts, c), F32),
                        pltpu.VMEM((SUBLANES_F32, c), F32)],
        compiler_params=pltpu.CompilerParams(
            dimension_semantics=("arbitrary", "arbitrary"),
            vmem_limit_bytes=VMEM_LIMIT_BYTES),
        name="rglru",
    )(xr, gr, cw, cb, wa, ba, wx, bx, lam, nw)


def _mlp_kernel(attn_ref, y_ref, x_ref, wo_ref, n2_ref, wu_ref, cw_ref, cb_ref, wd_ref,
                o_ref, ug_ref, uv_ref):
    tm = FFN_ROWS
    hr = SUBLANES_F32

    @pl.when(pl.program_id(1) == 0)
    def _():
        ug_ref[0:hr, :] = jnp.zeros((hr, D_FF), F32)
        uv_ref[0:hr, :] = jnp.zeros((hr, D_FF), F32)

    x1 = x_ref[0] + jnp.dot(attn_ref[0, 0], wo_ref[0:ATTN_WIDTH, :],
                            preferred_element_type=F32)
    x1 = x1 + jnp.dot(y_ref[0], wo_ref[ATTN_WIDTH:, :], preferred_element_type=F32)
    ms = jnp.mean(x1 * x1, axis=-1, keepdims=True)
    h2 = (x1 * lax.rsqrt(ms + EPS) * n2_ref[...]).astype(BF16)

    def conv(u_ref, c0, scale=1.0):
        cw = cw_ref[:, c0:c0 + D_FF] * scale
        y = cb_ref[:, c0:c0 + D_FF] * scale + (
            cw[FFN_CONV_WIDTH - 1:FFN_CONV_WIDTH, :] * u_ref[hr:hr + tm, :])
        for d in range(1, FFN_CONV_WIDTH):
            tap = FFN_CONV_WIDTH - 1 - d
            y = y + cw[tap:tap + 1, :] * u_ref[hr - d:hr - d + tm, :]
        return y

    ug_ref[hr:hr + tm, :] = jnp.dot(h2, wu_ref[:, 0:D_FF], preferred_element_type=F32)
    uv_ref[hr:hr + tm, :] = jnp.dot(h2, wu_ref[:, D_FF:], preferred_element_type=F32)
    act = _gelu_gate(conv(ug_ref, 0), conv(uv_ref, D_FF, 0.5)).astype(BF16)
    ug_ref[0:hr, :] = ug_ref[tm:tm + hr, :]
    uv_ref[0:hr, :] = uv_ref[tm:tm + hr, :]
    o_ref[0] = x1 + jnp.dot(act, wd_ref[...], preferred_element_type=F32)


def _mlp(attn, y, x, w_out, n2, w_up, cw, cb, w_down):
    b, s, d = x.shape
    tm = FFN_ROWS
    hr = SUBLANES_F32
    row_blk = lambda w: pl.BlockSpec((1, tm, w), lambda bi, i: (bi, i, 0))
    ratio = ATTN_BLOCK // tm
    nq = s // ATTN_BLOCK
    nh = nq // 2

    def fold(i):
        qb = i // ratio
        return qb // nh, jnp.where(qb < nh, qb, nq - 1 - qb) * ratio + i % ratio

    attn_blk = pl.BlockSpec((1, 1, tm, ATTN_WIDTH), lambda bi, i: (bi, *fold(i), 0))
    resident = lambda shape: pl.BlockSpec(shape, lambda bi, i: (0,) * len(shape),
                                          pipeline_mode=pl.Buffered(1))
    return pl.pallas_call(
        _mlp_kernel,
        grid=(b, s // tm),
        in_specs=[attn_blk, row_blk(RNN_WIDTH), row_blk(d),
                  resident((d, d)), resident((1, d)), resident((d, 2 * D_FF)),
                  resident((FFN_CONV_WIDTH, 2 * D_FF)), resident((1, 2 * D_FF)),
                  resident((D_FF, d))],
        out_specs=row_blk(d),
        out_shape=jax.ShapeDtypeStruct((b, s, d), F32),
        scratch_shapes=[pltpu.VMEM((hr + tm, D_FF), F32)] * 2,
        compiler_params=pltpu.CompilerParams(
            dimension_semantics=("arbitrary", "arbitrary"),
            vmem_limit_bytes=VMEM_LIMIT_BYTES),
        name="mlp",
    )(attn, y, x, w_out, n2, w_up, cw, cb, w_down)


def _rope_tables(seq):
    f32 = np.float32
    pos = np.arange(seq, dtype=f32)
    inv_freq = f32(ROPE_THETA) ** (-(np.arange(0, ROT_DIM, 2, dtype=f32) / f32(ROT_DIM)))
    ang = pos[:, None] * inv_freq[None, :].astype(f32)
    cos, sin = np.cos(ang).astype(f32), np.sin(ang).astype(f32)
    half = ROT_DIM // 2
    pad = QK_HEAD_DIM - ROT_DIM
    one = np.ones((seq, pad), f32)
    zero_h = np.zeros((seq, half), f32)
    zero_p = np.zeros((seq, pad), f32)
    rc = np.concatenate([cos, cos, one], axis=1)
    rs1 = np.concatenate([-sin, zero_h, zero_p], axis=1)
    rs2 = np.concatenate([zero_h, sin, zero_p], axis=1)
    tables = (cos.T, sin.T) + tuple(np.tile(t, (1, 2)) for t in (rc, rs1, rs2))
    return tuple(jnp.asarray(np.ascontiguousarray(t)) for t in tables)


def _block_diag(w):
    h, i, j = w.shape
    eye = jnp.eye(h, dtype=w.dtype)
    return (eye[:, None, :, None] * w[:, :, None, :]).reshape(h * i, h * j)


def kernel(x, norm1_w, w_in, q_norm_w, k_norm_w, lambda_q1, lambda_k1, lambda_q2, lambda_k2,
           subln_w, conv_rnn_w, conv_rnn_b, w_gate_a, b_gate_a, w_gate_x, b_gate_x, lru_lambda,
           rnn_norm_w, w_out, norm2_w, w_up, conv_ffn_w, conv_ffn_b, w_down):
    b, s, d = x.shape
    ct, st, rc, rs1, rs2 = _rope_tables(s)
    row = lambda v: v.reshape(1, -1)
    for l in range(DEPTH):
        lambda_init = 0.8 - 0.6 * math.exp(-0.3 * l)
        w = w_in[l].astype(BF16)
        v0 = 2 * QK_WIDTH
        wt = jnp.concatenate([w[:, :QK_WIDTH], w[:, v0:v0 + ATTN_WIDTH]], axis=1).T
        w_rest = jnp.concatenate([w[:, QK_WIDTH:v0], w[:, v0 + ATTN_WIDTH:]], axis=1)
        qt, k, vt, xr, gr, w_out16, w_up16, w_down16 = _in_proj(
            x, row(norm1_w[l]), wt, w_rest, q_norm_w[l].reshape(-1, 1),
            row(jnp.tile(k_norm_w[l], 2)), ct, st, rc, rs1, rs2,
            (w_out[l], w_up[l], w_down[l]))
        attn = _attention(qt, k, vt, row(lambda_q1[l]), row(lambda_k1[l]), row(lambda_q2[l]),
                          row(lambda_k2[l]), row(subln_w[l]), lambda_init)
        y = _rglru(xr, gr, conv_rnn_w[l], row(conv_rnn_b[l]),
                   _block_diag(w_gate_a[l]).astype(BF16), row(b_gate_a[l]),
                   _block_diag(w_gate_x[l]).astype(BF16), row(b_gate_x[l]),
                   row(lru_lambda[l]), row(rnn_norm_w[l]))
        x = _mlp(attn, y, x, w_out16, row(norm2_w[l]), w_up16,
                 conv_ffn_w[l], row(conv_ffn_b[l]), w_down16)
    return x
```
